```python
import math
import numpy as np
import jax
import jax.numpy as jnp
from jax import lax

D_MODEL = 1024
BATCH = 8
SEQ = 2048
DEPTH = 4
DEC_BATCH = 32
DEC_SEQ = 1
PAST_LEN = 16384
PAGE_SIZE = 128

H_A = 4
KV_A = 2
DA = 64
ROT_A = DA // 4
ROPE_THETA = 500000.0
H_B = 8
Q_RANK_B = 256
KV_RANK_B = 128
DN_B = 64
DR_B = 32
DV_B = 64
ROPE_THETA_MLA = 10000.0
H_C = 8
KV_C = 2
DC = 64
N_BRANCH = 3
BRANCH_W = 512
D_FF = 2816
Q_BLOCK = 128
EPS = 1e-6
POOL_NUM = 5
POOL_DEN = 4

IN_WIDTHS = (H_A * 2 * DA, KV_A * 2 * DA, KV_A * 2 * DA, Q_RANK_B, KV_RANK_B, DR_B,
             H_C * DC, KV_C * DC, KV_C * DC, H_C, N_BRANCH * D_MODEL)
N_IN = (H_A * 2 * DA + 2 * KV_A * 2 * DA + Q_RANK_B + KV_RANK_B + DR_B
        + H_C * DC + 2 * KV_C * DC + H_C + N_BRANCH * D_MODEL)
CACHE_NAMES = ('a_k', 'a_v', 'b_lat', 'b_kr', 'c_k', 'c_v', 'c_logf')
ATTN_KEYS = ('a_k', 'a_v', 'b_lat', 'b_kr', 'c_k', 'c_v')

kernel_name = 'hybrid_diff_mla_fox_macaron_step'


def _rms(x, g):
    xf = x.astype(jnp.float32)
    y = xf * lax.rsqrt(jnp.mean(xf * xf, axis=-1, keepdims=True) + EPS)
    return (y * g.astype(jnp.float32)).astype(x.dtype)


def _rope(x, pos, theta):
    d = x.shape[-1]
    inv = 1.0 / (theta ** (jnp.arange(0, d, 2, dtype=jnp.float32) / d))
    ang = pos.astype(jnp.float32)[:, None] * inv[None, :]
    cos = jnp.cos(ang)[None, :, None, :]
    sin = jnp.sin(ang)[None, :, None, :]
    xf = x.astype(jnp.float32)
    x1, x2 = xf[..., : d // 2], xf[..., d // 2:]
    return jnp.concatenate([x1 * cos - x2 * sin, x2 * cos + x1 * sin], axis=-1).astype(x.dtype)


def _partial_rope(x, pos):
    return jnp.concatenate([_rope(x[..., :ROT_A], pos, ROPE_THETA), x[..., ROT_A:]], axis=-1)


def _swiglu(x, wg, wu, wd):
    return (jax.nn.silu(x @ wg) * (x @ wu)) @ wd


def _causal_softmax(s, qpos, kpos):
    valid = kpos[None, :] <= qpos[:, None]
    s = jnp.where(valid, s, jnp.finfo(jnp.float32).min)
    return jax.nn.softmax(s, axis=-1)


def _sweep(fn, q_args):
    nq = q_args[0].shape[1]
    if nq <= Q_BLOCK or nq % Q_BLOCK:
        return fn(*q_args)
    nb = nq // Q_BLOCK

    def split(a):
        return jnp.moveaxis(a.reshape(a.shape[0], nb, Q_BLOCK, *a.shape[2:]), 1, 0)

    out = lax.map(lambda blk: fn(*blk), tuple(split(a) for a in q_args))
    out = jnp.moveaxis(out, 0, 1)
    return out.reshape(out.shape[0], nq, *out.shape[3:])


def _diff_core(q, qpos, k, v, kpos, lam):
    b, nq = q.shape[:2]
    nk = k.shape[1]
    qg = q.reshape(b, nq, KV_A, H_A // KV_A, 2, DA)
    kg = k.reshape(b, nk, KV_A, 2, DA)
    s = jnp.einsum('bqgrmd,bkgmd->mbgrqk', qg, kg, preferred_element_type=jnp.float32) * (DA ** -0.5)
    p = _causal_softmax(s, qpos, kpos)
    w = (p[0] - lam * p[1]).astype(v.dtype)
    o = jnp.einsum('bgrqk,bkge->bqgre', w, v)
    return o.reshape(b, nq, H_A, 2 * DA)


def _mla_core(q_lat, q_rope, qpos, c_kv, k_rope, kpos):
    s = (jnp.einsum('bqhc,bkc->bhqk', q_lat, c_kv, preferred_element_type=jnp.float32)
         + jnp.einsum('bqhr,bkr->bhqk', q_rope, k_rope, preferred_element_type=jnp.float32)) * ((DN_B + DR_B) ** -0.5)
    p = _causal_softmax(s, qpos, kpos).astype(c_kv.dtype)
    return jnp.einsum('bhqk,bkc->bqhc', p, c_kv)


def _fox_core(q, cq, qpos, k, v, ck, kpos):
    b, nq = q.shape[:2]
    nk = k.shape[1]
    r = H_C // KV_C
    qg = q.reshape(b, nq, KV_C, r, DC)
    s = jnp.einsum('bqgrd,bkgd->bgrqk', qg, k, preferred_element_type=jnp.float32) * (DC ** -0.5)
    cq_g = jnp.transpose(cq.reshape(b, nq, KV_C, r), (0, 2, 3, 1))
    ck_g = jnp.transpose(ck.reshape(b, nk, KV_C, r), (0, 2, 3, 1))
    s = s + cq_g[..., :, None] - ck_g[..., None, :]
    p = _causal_softmax(s, qpos, kpos).astype(v.dtype)
    o = jnp.einsum('bgrqk,bkgd->bqgrd', p, v)
    return o.reshape(b, nq, H_C, DC)


def _gather(cache_l, page_table):
    pages = jnp.take(cache_l, page_table, axis=0)
    return pages.reshape(page_table.shape[0], page_table.shape[1] * cache_l.shape[1], *cache_l.shape[2:])


def _mixer(h, past, lp, layer_idx):
    b, L, _ = h.shape
    z = h @ lp['w_in']
    offs = np.cumsum(IN_WIDTHS)[:-1].tolist()
    a_q, a_k, a_v, b_cq, b_ckv, b_kr, c_q, c_k, c_v, c_f, gate = jnp.split(z, offs, axis=-1)
    past_len = 0 if past is None else past['a_k'].shape[1]
    kpos = jnp.arange(past_len + L, dtype=jnp.int32)
    qpos = kpos[past_len:]
    a_q = _partial_rope(a_q.reshape(b, L, H_A * 2, DA), qpos).reshape(b, L, H_A, 2 * DA)
    a_k = _partial_rope(a_k.reshape(b, L, KV_A * 2, DA), qpos).reshape(b, L, KV_A, 2 * DA)
    a_v = a_v.reshape(b, L, KV_A, 2 * DA)
    q_b = (_rms(b_cq, lp['mla_q_norm']) @ lp['mla_w_uq']).reshape(b, L, H_B, DN_B + DR_B)
    q_lat = jnp.einsum('blhn,chn->blhc', q_b[..., :DN_B], lp['mla_w_uk'])
    q_rope = _rope(q_b[..., DN_B:], qpos, ROPE_THETA_MLA)
    c_kv = _rms(b_ckv, lp['mla_kv_norm'])
    k_rope = _rope(b_kr[:, :, None, :], qpos, ROPE_THETA_MLA)[:, :, 0]
    c_q = c_q.reshape(b, L, H_C, DC)
    c_k = c_k.reshape(b, L, KV_C, DC)
    c_v = c_v.reshape(b, L, KV_C, DC)
    logf = jax.nn.log_sigmoid(c_f.astype(jnp.float32) + lp['b_forget'].astype(jnp.float32))
    rows = {'a_k': a_k, 'a_v': a_v, 'b_lat': c_kv, 'b_kr': k_rope,
            'c_k': c_k, 'c_v': c_v, 'c_logf': logf.astype(h.dtype)}
    if past is None:
        keys = rows
        logf_all = logf
    else:
        keys = {n: jnp.concatenate([past[n], rows[n]], axis=1) for n in ATTN_KEYS}
        logf_all = jnp.concatenate([past['c_logf'].astype(jnp.float32), logf], axis=1)
    cum = jnp.cumsum(logf_all, axis=1)
    qp = qpos[None, :]
    lam_init = 0.8 - 0.6 * math.exp(-0.3 * layer_idx)
    dl = lp['diff_lambda'].astype(jnp.float32)
    lam = jnp.exp(jnp.sum(dl[0] * dl[1])) - jnp.exp(jnp.sum(dl[2] * dl[3])) + lam_init
    y_a = _sweep(lambda q, qq: _diff_core(q, qq[0], keys['a_k'], keys['a_v'], kpos, lam), (a_q, qp))
    y_a = (_rms(y_a, lp['diff_subln']) * (1.0 - lam_init)).reshape(b, L, BRANCH_W)
    y_b = _sweep(lambda ql, qr, qq: _mla_core(ql, qr, qq[0], keys['b_lat'], keys['b_kr'], kpos), (q_lat, q_rope, qp))
    y_b = jnp.einsum('blhc,chv->blhv', y_b, lp['mla_w_uv']).reshape(b, L, BRANCH_W)
    y_c = _sweep(lambda q, cq, qq: _fox_core(q, cq, qq[0], keys['c_k'], keys['c_v'], cum, kpos),
                 (c_q, cum[:, past_len:], qp))
    y_c = y_c.reshape(b, L, BRANCH_W)
    gates = jax.nn.sigmoid(gate.astype(jnp.float32)).reshape(b, L, N_BRANCH, D_MODEL)
    br = jnp.stack([y_a, y_b, y_c], axis=2)
    proj = jnp.einsum('blnw,nwd->blnd', br, lp['w_branch'], preferred_element_type=jnp.float32)
    merged = jnp.sum(gates * proj, axis=2).astype(h.dtype)
    return merged @ lp['w_out'], rows


def _layer(x, past, lp, layer_idx):
    g = lp['norm_g']
    h = x + 0.5 * _rms(_swiglu(_rms(x, g[0]), lp['ffn_w_gate'][0], lp['ffn_w_up'][0], lp['ffn_w_down'][0]), g[1])
    m, rows = _mixer(_rms(h, g[2]), past, lp, layer_idx)
    h = h + _rms(m, g[3])
    h = h + 0.5 * _rms(_swiglu(_rms(h, g[4]), lp['ffn_w_gate'][1], lp['ffn_w_up'][1], lp['ffn_w_down'][1]), g[5])
    return h, rows


def setup_inputs(seed: int = 0) -> dict:
    key = jax.random.key(seed)
    ks = jax.random.split(key, 26)
    n_pages = PAST_LEN // PAGE_SIZE
    n_used = DEC_BATCH * n_pages
    n_pool = (POOL_NUM * n_used + POOL_DEN - 1) // POOL_DEN
    pool = (DEPTH, n_pool, PAGE_SIZE)

    def nrm(k, shape, scale=1.0):
        return jax.random.normal(k, shape, jnp.float32) * scale

    return {
        'x_prompt': nrm(ks[0], (BATCH, SEQ, D_MODEL)),
        'x_sample': nrm(ks[1], (DEC_BATCH, DEC_SEQ, D_MODEL)),
        'cache_a_k': nrm(ks[2], pool + (KV_A, 2 * DA)),
        'cache_a_v': nrm(ks[3], pool + (KV_A, 2 * DA)),
        'cache_b_lat': nrm(ks[4], pool + (KV_RANK_B,)),
        'cache_b_kr': nrm(ks[5], pool + (DR_B,)),
        'cache_c_k': nrm(ks[6], pool + (KV_C, DC)),
        'cache_c_v': nrm(ks[7], pool + (KV_C, DC)),
        'cache_c_logf': jax.nn.log_sigmoid(2.0 + nrm(ks[8], pool + (H_C,))),
        'page_table': jax.random.permutation(ks[9], n_pool)[:n_used].reshape(DEC_BATCH, n_pages).astype(jnp.int32),
        'norm_g': 1.0 + nrm(ks[10], (DEPTH, 6, D_MODEL), 0.05),
        'ffn_w_gate': nrm(ks[11], (DEPTH, 2, D_MODEL, D_FF), D_MODEL ** -0.5),
        'ffn_w_up': nrm(ks[12], (DEPTH, 2, D_MODEL, D_FF), D_MODEL ** -0.5),
        'ffn_w_down': nrm(ks[13], (DEPTH, 2, D_FF, D_MODEL), D_FF ** -0.5),
        'w_in': nrm(ks[14], (DEPTH, D_MODEL, N_IN), D_MODEL ** -0.5),
        'b_forget': 2.0 + nrm(ks[15], (DEPTH, H_C), 0.5),
        'diff_lambda': nrm(ks[16], (DEPTH, 4, DA), 0.1),
        'diff_subln': 1.0 + nrm(ks[17], (DEPTH, 2 * DA), 0.05),
        'mla_q_norm': 1.0 + nrm(ks[18], (DEPTH, Q_RANK_B), 0.05),
        'mla_kv_norm': 1.0 + nrm(ks[19], (DEPTH, KV_RANK_B), 0.05),
        'mla_w_uq': nrm(ks[20], (DEPTH, Q_RANK_B, H_B * (DN_B + DR_B)), Q_RANK_B ** -0.5),
        'mla_w_uk': nrm(ks[21], (DEPTH, KV_RANK_B, H_B, DN_B), KV_RANK_B ** -0.5),
        'mla_w_uv': nrm(ks[22], (DEPTH, KV_RANK_B, H_B, DV_B), KV_RANK_B ** -0.5),
        'w_branch': nrm(ks[23], (DEPTH, N_BRANCH, BRANCH_W, D_MODEL), BRANCH_W ** -0.5),
        'w_out': nrm(ks[24], (DEPTH, D_MODEL, D_MODEL), D_MODEL ** -0.5),
    }


def reference(x_prompt, x_sample, cache_a_k, cache_a_v, cache_b_lat, cache_b_kr, cache_c_k, cache_c_v,
              cache_c_logf, page_table, norm_g, ffn_w_gate, ffn_w_up, ffn_w_down, w_in, b_forget,
              diff_lambda, diff_subln, mla_q_norm, mla_kv_norm, mla_w_uq, mla_w_uk, mla_w_uv,
              w_branch, w_out):
    caches = {'a_k': cache_a_k, 'a_v': cache_a_v, 'b_lat': cache_b_lat, 'b_kr': cache_b_kr,
              'c_k': cache_c_k, 'c_v': cache_c_v, 'c_logf': cache_c_logf}
    yp, ys = x_prompt, x_sample
    new_p = {n: [] for n in CACHE_NAMES}
    new_s = {n: [] for n in CACHE_NAMES}
    for l in range(DEPTH):
        lp = {'norm_g': norm_g[l], 'ffn_w_gate': ffn_w_gate[l], 'ffn_w_up': ffn_w_up[l],
              'ffn_w_down': ffn_w_down[l], 'w_in': w_in[l], 'b_forget': b_forget[l],
              'diff_lambda': diff_lambda[l], 'diff_subln': diff_subln[l],
              'mla_q_norm': mla_q_norm[l], 'mla_kv_norm': mla_kv_norm[l], 'mla_w_uq': mla_w_uq[l],
              'mla_w_uk': mla_w_uk[l], 'mla_w_uv': mla_w_uv[l], 'w_branch': w_branch[l], 'w_out': w_out[l]}
        yp, rows_p = _layer(yp, None, lp, l)
        past = {n: _gather(caches[n][l], page_table) for n in CACHE_NAMES}
        ys, rows_s = _layer(ys, past, lp, l)
        for n in CACHE_NAMES:
            new_p[n].append(rows_p[n])
            new_s[n].append(rows_s[n])
    return (yp, ys,
            jnp.stack(new_p['a_k']), jnp.stack(new_s['a_k']),
            jnp.stack(new_p['a_v']), jnp.stack(new_s['a_v']),
            jnp.stack(new_p['b_lat']), jnp.stack(new_s['b_lat']),
            jnp.stack(new_p['b_kr']), jnp.stack(new_s['b_kr']),
            jnp.stack(new_p['c_k']), jnp.stack(new_s['c_k']),
            jnp.stack(new_p['c_v']), jnp.stack(new_s['c_v']),
            jnp.stack(new_p['c_logf']), jnp.stack(new_s['c_logf']))
```

```python
import functools
import math

import numpy as np
import jax
import jax.numpy as jnp
from jax import lax
from jax.experimental import pallas as pl
from jax.experimental.pallas import tpu as pltpu

F32 = jnp.float32
BF16 = jnp.bfloat16

D_MODEL = 1024
D_FF = 2816
EPS = 1e-6
PAGE = 128
H_A, KV_A, DA, ROT_A, THETA_A = 4, 2, 64, 16, 500000.0
H_B, Q_RANK_B, KV_RANK_B, DN_B, DR_B, DV_B, THETA_B = 8, 256, 128, 64, 32, 64, 10000.0
H_C, KV_C, DC = 8, 2, 64
BRANCH_W = 512
N_BRANCH = 3
LANES = 128
NEG = -1e30

O_AQ, O_AK, O_AV, O_BCQ, O_CKV, O_CK, O_CQ, O_CV, O_KR, O_CF, N_MAIN = (
    0, 512, 768, 1024, 1280, 1408, 1536, 2048, 2176, 2304, 2432)

FF_CHUNK = 1408
TQ = 256
VMEM_LIMIT = 56 * 1024 * 1024
PAGES_PER_STEP = 4


def _cparams(sem):
    return pltpu.CompilerParams(dimension_semantics=sem, vmem_limit_bytes=VMEM_LIMIT)


def _rms(x, g):
    return x * lax.rsqrt(jnp.mean(x * x, axis=-1, keepdims=True) + EPS) * g


def _log_sigmoid(x):
    return jnp.minimum(x, 0.0) - jnp.log(1.0 + jnp.exp(-jnp.abs(x)))


def _dot(a, b):
    return jnp.dot(a, b, preferred_element_type=F32)


def _dot_nt(a, b):
    return lax.dot_general(a, b, (((1,), (1,)), ((), ())), preferred_element_type=F32)


def _ffn_body(x_ref, g_ref, wg_ref, wu_ref, wd_ref, o_ref, *, pre, post):
    x = x_ref[...]
    xn = _rms(x, g_ref[pre:pre + 1, :]).astype(BF16)
    acc = None
    for c in range(D_FF // FF_CHUNK):
        sl = slice(c * FF_CHUNK, (c + 1) * FF_CHUNK)
        g = _dot(xn, wg_ref[:, sl])
        u = _dot(xn, wu_ref[:, sl])
        h = (g * jax.nn.sigmoid(g) * u).astype(BF16)
        d = _dot(h, wd_ref[sl, :])
        acc = d if acc is None else acc + d
    o_ref[...] = x + 0.5 * _rms(acc, g_ref[post:post + 1, :])


def _ffn(x, norm_g, wg, wu, wd, l, j, tm):
    t = x.shape[0]
    once = pl.Buffered(1)
    return pl.pallas_call(
        functools.partial(_ffn_body, pre=3 * j + (0 if j == 0 else 1), post=3 * j + (1 if j == 0 else 2)),
        grid=(t // tm,),
        in_specs=[
            pl.BlockSpec((tm, D_MODEL), lambda i: (i, 0)),
            pl.BlockSpec((None, 6, D_MODEL), lambda i: (l, 0, 0)),
            pl.BlockSpec((None, None, D_MODEL, D_FF), lambda i: (l, j, 0, 0), pipeline_mode=once),
            pl.BlockSpec((None, None, D_MODEL, D_FF), lambda i: (l, j, 0, 0), pipeline_mode=once),
            pl.BlockSpec((None, None, D_FF, D_MODEL), lambda i: (l, j, 0, 0), pipeline_mode=once),
        ],
        out_specs=pl.BlockSpec((tm, D_MODEL), lambda i: (i, 0)),
        out_shape=jax.ShapeDtypeStruct((t, D_MODEL), F32),
        compiler_params=_cparams(("parallel",)),
        name="ffn",
    )(x, norm_g, wg, wu, wd)


def _rope(x, tab_ref, half):
    return (x * tab_ref[0]
            + pltpu.roll(x, LANES - half, 1) * tab_ref[1]
            + pltpu.roll(x, half, 1) * tab_ref[2])


def _inproj_body(h_ref, g_ref, wm_ref, wft_ref, bf_ref, bft_ref, qn_ref, kvn_ref, wuq_ref, wuk_ref,
                 ta_ref, tb_ref,
                 ak_ref, av_ref, ckv_ref, kr_ref, ck_ref, cv_ref, lf_ref, lft_ref,
                 aqb_ref, akb_ref, avb_ref, qb_ref, kb_ref, cqb_ref, ckb_ref, cvb_ref):
    hn = _rms(h_ref[...], g_ref[2:3, :]).astype(BF16)
    z = _dot(hn, wm_ref[...])

    def slab(off, i=0):
        return z[:, off + i * LANES: off + (i + 1) * LANES]

    for i in range(4):
        aqb_ref[:, i * LANES:(i + 1) * LANES] = (_rope(slab(O_AQ, i), ta_ref, ROT_A // 2) * (DA ** -0.5)).astype(BF16)
    for i in range(2):
        k = _rope(slab(O_AK, i), ta_ref, ROT_A // 2)
        ak_ref[:, i * LANES:(i + 1) * LANES] = k
        akb_ref[:, i * LANES:(i + 1) * LANES] = k.astype(BF16)
    av = z[:, O_AV:O_AV + 2 * LANES]
    av_ref[...] = av
    avb_ref[...] = av.astype(BF16)

    qc = _rms(z[:, O_BCQ:O_BCQ + Q_RANK_B], qn_ref[...]).astype(BF16)
    qb = _dot(qc, wuq_ref[...])
    q_lat = _dot(qb[:, :H_B * DN_B].astype(BF16), wuk_ref[...])
    scale_b = (DN_B + DR_B) ** -0.5
    lane = lax.broadcasted_iota(jnp.int32, (1, LANES), 1)
    for s in range(2):
        qr = _rope(qb[:, H_B * DN_B + s * LANES: H_B * DN_B + (s + 1) * LANES], tb_ref, DR_B // 2) * scale_b
        for hh in range(4):
            h = s * 4 + hh
            qb_ref[:, h * 256: h * 256 + LANES] = (q_lat[:, h * LANES:(h + 1) * LANES] * scale_b).astype(BF16)
            sel = (lane >= hh * DR_B) & (lane < (hh + 1) * DR_B)
            qb_ref[:, h * 256 + LANES: (h + 1) * 256] = jnp.where(sel, qr, 0.0).astype(BF16)
    ckv = _rms(slab(O_CKV), kvn_ref[...])
    ckv_ref[...] = ckv
    kr = _rope(slab(O_KR), tb_ref, DR_B // 2)
    kr_ref[...] = kr[:, :DR_B]
    kb_ref[:, :LANES] = ckv.astype(BF16)
    kb_ref[:, LANES:] = kr.astype(BF16)

    cqb_ref[...] = (z[:, O_CQ:O_CQ + H_C * DC] * (DC ** -0.5)).astype(BF16)
    ck = slab(O_CK)
    ck_ref[...] = ck
    ckb_ref[...] = ck.astype(BF16)
    cv = slab(O_CV)
    cv_ref[...] = cv
    cvb_ref[...] = cv.astype(BF16)
    lf = _log_sigmoid(slab(O_CF) + bf_ref[...])
    lf_ref[...] = lf[:, :H_C]
    lft_ref[...] = _log_sigmoid(_dot_nt(wft_ref[...], hn) + bft_ref[...])


def _inproj(h, norm_g, p, l, tab_a, tab_b, tm):
    t = h.shape[0]
    n_pos = tab_a.shape[1] // tm

    def full(a):
        nd = a.ndim - 1
        return pl.BlockSpec((None,) + a.shape[1:], lambda i: (l,) + (0,) * nd)

    def rows(w):
        return pl.BlockSpec((tm, w), lambda i: (i, 0))

    tab_spec = pl.BlockSpec((3, tm, LANES), lambda i: (0, i % n_pos, 0))
    f32_w = (256, 256, 128, DR_B, 128, 128, H_C)
    bf_w = (512, 256, 256, H_B * 256, 256, 512, 128, 128)
    out_shape = ([jax.ShapeDtypeStruct((t, w), F32) for w in f32_w] + [jax.ShapeDtypeStruct((H_C, t), F32)]
                 + [jax.ShapeDtypeStruct((t, w), BF16) for w in bf_w])
    out_specs = ([rows(w) for w in f32_w] + [pl.BlockSpec((H_C, tm), lambda i: (0, i))] + [rows(w) for w in bf_w])
    return pl.pallas_call(
        _inproj_body,
        grid=(t // tm,),
        in_specs=[rows(D_MODEL), full(norm_g), full(p["w_main"]), full(p["w_ft"]), full(p["b_f"]), full(p["b_ft"]),
                  full(p["q_norm"]), full(p["kv_norm"]), full(p["w_uq"]), full(p["w_uk_bd"]), tab_spec, tab_spec],
        out_specs=out_specs,
        out_shape=out_shape,
        compiler_params=_cparams(("parallel",)),
        name="inproj",
    )(h, norm_g, p["w_main"], p["w_ft"], p["b_f"], p["b_ft"], p["q_norm"], p["kv_norm"], p["w_uq"], p["w_uk_bd"],
      tab_a, tab_b)


def _cumsum_body(lf_ref, lft_ref, cr_ref, ct_ref):
    x = lf_ref[...]
    n = x.shape[0]
    row = lax.broadcasted_iota(jnp.int32, x.shape, 0)
    s = 1
    while s < n:
        x = x + jnp.where(row >= s, pltpu.roll(x, s, 0), 0.0)
        s *= 2
    cr_ref[...] = x
    y = lft_ref[...]
    col = lax.broadcasted_iota(jnp.int32, y.shape, 1)
    s = 1
    while s < n:
        y = y + jnp.where(col >= s, pltpu.roll(y, s, 1), 0.0)
        s *= 2
    ct_ref[...] = y


def _cumsum(lf, lft, batch, seq):
    t = lf.shape[0]
    return pl.pallas_call(
        _cumsum_body,
        grid=(batch,),
        in_specs=[pl.BlockSpec((seq, H_C), lambda b: (b, 0)), pl.BlockSpec((H_C, seq), lambda b: (0, b))],
        out_specs=[pl.BlockSpec((seq, H_C), lambda b: (b, 0)), pl.BlockSpec((H_C, seq), lambda b: (0, b))],
        out_shape=[jax.ShapeDtypeStruct((t, H_C), F32), jax.ShapeDtypeStruct((H_C, t), F32)],
        compiler_params=_cparams(("parallel",)),
        name="cumsum",
    )(lf, lft)


def _flash_head(q, k_ref, kcols, v_ref, vcols, qi, bias=None):
    tq = q.shape[0]
    dv = vcols.stop - vcols.start

    def block(j, carry, diagonal):
        m, l, acc = carry
        r0 = pl.multiple_of(j * TQ, TQ)
        s = _dot_nt(q, k_ref[pl.ds(r0, TQ), kcols])
        if bias is not None:
            s = s + bias(r0)
        if diagonal:
            rr = lax.broadcasted_iota(jnp.int32, s.shape, 0)
            cc = lax.broadcasted_iota(jnp.int32, s.shape, 1)
            s = jnp.where(cc <= rr, s, NEG)
        m_new = jnp.maximum(m, jnp.max(s, axis=-1, keepdims=True))
        alpha = jnp.exp(m - m_new)
        pr = jnp.exp(s - m_new)
        l = alpha * l + jnp.sum(pr, axis=-1, keepdims=True)
        acc = alpha * acc + _dot(pr.astype(BF16), v_ref[pl.ds(r0, TQ), vcols])
        return m_new, l, acc

    init = (jnp.full((tq, 1), NEG, F32), jnp.zeros((tq, 1), F32), jnp.zeros((tq, dv), F32))
    carry = lax.fori_loop(0, qi, lambda j, c: block(j, c, False), init)
    m, l, acc = block(qi, carry, True)
    return acc / l


def _flash_a_body(q_ref, k_ref, v_ref, dl_ref, sub_ref, o_ref, *, lam_init):
    qi = pl.program_id(1)
    dl = dl_ref[...]
    lam = (jnp.exp(jnp.sum(dl[0:1] * dl[1:2], axis=-1, keepdims=True))
           - jnp.exp(jnp.sum(dl[2:3] * dl[3:4], axis=-1, keepdims=True)) + lam_init)
    for h in range(H_A):
        g = h // (H_A // KV_A)
        o = []
        for mp in range(2):
            q = q_ref[:, h * 2 * DA + mp * DA: h * 2 * DA + (mp + 1) * DA]
            o.append(_flash_head(q, k_ref, slice(g * 2 * DA + mp * DA, g * 2 * DA + (mp + 1) * DA),
                                 v_ref, slice(g * 2 * DA, (g + 1) * 2 * DA), qi))
        y = o[0] - lam * o[1]
        o_ref[:, h * 2 * DA:(h + 1) * 2 * DA] = (_rms(y, sub_ref[...]) * (1.0 - lam_init)).astype(BF16)


def _flash_b_body(q_ref, k_ref, wuv_ref, o_ref):
    qi = pl.program_id(1)
    o = []
    for h in range(H_B):
        q = q_ref[:, h * 256:(h + 1) * 256]
        o.append(_flash_head(q, k_ref, slice(0, 256), k_ref, slice(0, KV_RANK_B), qi).astype(BF16))
    o_ref[...] = _dot(jnp.concatenate(o, axis=1), wuv_ref[...]).astype(BF16)


def _flash_c_body(q_ref, k_ref, v_ref, cr_ref, ct_ref, o_ref):
    qi = pl.program_id(1)
    cr = cr_ref[...]
    for h in range(H_C):
        g = h // (H_C // KV_C)
        q = q_ref[:, h * DC:(h + 1) * DC]
        cq = cr[:, h:h + 1]

        def bias(r0, cq=cq, h=h):
            return cq - ct_ref[h:h + 1, pl.ds(r0, TQ)]

        o = _flash_head(q, k_ref, slice(g * DC, (g + 1) * DC), v_ref, slice(g * DC, (g + 1) * DC), qi, bias)
        o_ref[:, h * DC:(h + 1) * DC] = o.astype(BF16)


def _flash_call(body, name, batch, seq, q, kv, extra, extra_specs):
    t = q.shape[0]
    nq = seq // TQ
    q_spec = pl.BlockSpec((TQ, q.shape[1]), lambda b, i: (b * nq + i, 0))
    kv_specs = [pl.BlockSpec((seq, a.shape[1]), lambda b, i: (b, 0)) for a in kv]
    return pl.pallas_call(
        body,
        grid=(batch, nq),
        in_specs=[q_spec] + kv_specs + extra_specs,
        out_specs=pl.BlockSpec((TQ, BRANCH_W), lambda b, i: (b * nq + i, 0)),
        out_shape=jax.ShapeDtypeStruct((t, BRANCH_W), BF16),
        compiler_params=_cparams(("parallel", "arbitrary")),
        name=name,
    )(q, *kv, *extra)


def _decode_body(pt_ref, *refs, lam_init, n_steps):
    n = PAGES_PER_STEP
    ak_p, av_p, lat_p, kr_p, ck_p, cv_p, lf_p = (refs[i * n:(i + 1) * n] for i in range(7))
    (qa_ref, ka_ref, va_ref, ql_ref, qr_ref, kl_ref, krn_ref, qc_ref, kc_ref, vc_ref, lfn_ref,
     dl_ref, sub_ref, wuv_ref) = refs[7 * n:7 * n + 14]
    oa_ref, ob_ref, oc_ref = refs[7 * n + 14:7 * n + 17]
    ma, la, acca, mb, lb, accb, mc, lc, accc, carry = refs[7 * n + 17:]
    step = pl.program_id(1)

    qa = qa_ref[...]
    ql = ql_ref[...]
    qr = qr_ref[...]
    qc = qc_ref[...]

    @pl.when(step == 0)
    def _():
        ma[...] = jnp.sum(qa.astype(F32) * ka_ref[...].astype(F32), axis=-1, keepdims=True)
        la[...] = jnp.ones_like(la)
        acca[...] = va_ref[...]
        mb[...] = (jnp.sum(ql.astype(F32) * kl_ref[...].astype(F32), axis=-1, keepdims=True)
                   + jnp.sum(qr.astype(F32) * krn_ref[...].astype(F32), axis=-1, keepdims=True))
        lb[...] = jnp.ones_like(lb)
        accb[...] = jnp.broadcast_to(kl_ref[...].astype(F32), accb.shape)
        mc[...] = jnp.sum(qc.astype(F32) * kc_ref[...].astype(F32), axis=-1, keepdims=True)
        lc[...] = jnp.ones_like(lc)
        accc[...] = vc_ref[...]
        carry[...] = lfn_ref[...]

    def update(s_list, m_ref, l_ref, acc_ref, pv):
        s = jnp.concatenate(s_list, axis=1)
        m_old = m_ref[...]
        m_new = jnp.maximum(m_old, jnp.max(s, axis=-1, keepdims=True))
        alpha = jnp.exp(m_old - m_new)
        pr = jnp.exp(s - m_new)
        l_ref[...] = alpha * l_ref[...] + jnp.sum(pr, axis=-1, keepdims=True)
        acc = alpha * acc_ref[...]
        for i in range(n):
            acc = acc + pv(i, pr[:, i * PAGE:(i + 1) * PAGE].astype(BF16))
        acc_ref[...] = acc
        m_ref[...] = m_new

    zero_b = jnp.zeros((), BF16)

    def by_group(x, shape):
        first = lax.broadcasted_iota(jnp.int32, shape, 0) < 4
        return jnp.where(first, x[0], x[1])

    def group_rows(pr, g):
        row = lax.broadcasted_iota(jnp.int32, pr.shape, 0)
        return jnp.where((row < 4) == (g == 0), pr, zero_b)

    s_a = [by_group([_dot_nt(qa, ak_p[i][:, g, :].astype(BF16)) for g in range(KV_A)], (8, PAGE)) for i in range(n)]

    def pv_a(i, pr):
        return sum(_dot(group_rows(pr, g), av_p[i][:, g, :].astype(BF16)) for g in range(KV_A))

    update(s_a, ma, la, acca, pv_a)

    lat = [lat_p[i][...].astype(BF16) for i in range(n)]
    s_b = [_dot_nt(ql, lat[i]) + _dot(qr, kr_p[i][...].astype(BF16)) for i in range(n)]
    update(s_b, mb, lb, accb, lambda i, pr: _dot(pr, lat[i]))

    lane = lax.broadcasted_iota(jnp.int32, (H_C, PAGE), 1)
    s_c = []
    run = carry[...]
    for i in range(n):
        x = lf_p[i][...]
        suf = x
        sh = 1
        while sh < PAGE:
            suf = suf + jnp.where(lane + sh < PAGE, pltpu.roll(suf, PAGE - sh, 1), 0.0)
            sh *= 2
        bias = (suf - x) + run
        run = run + suf[:, 0:1]
        s_c.append(by_group([_dot(qc, ck_p[i][g].astype(BF16)) for g in range(KV_C)], (8, PAGE)) + bias)
    carry[...] = run

    def pv_c(i, pr):
        return sum(_dot_nt(group_rows(pr, g), cv_p[i][g].astype(BF16)) for g in range(KV_C))

    update(s_c, mc, lc, accc, pv_c)

    @pl.when(step == n_steps - 1)
    def _():
        dl = dl_ref[...]
        lam = (jnp.exp(jnp.sum(dl[0:1] * dl[1:2], axis=-1, keepdims=True))
               - jnp.exp(jnp.sum(dl[2:3] * dl[3:4], axis=-1, keepdims=True)) + lam_init)
        oa = acca[...] / la[...]
        for h in range(H_A):
            y = oa[2 * h:2 * h + 1] - lam * oa[2 * h + 1:2 * h + 2]
            oa_ref[h:h + 1, :] = _rms(y, sub_ref[...]) * (1.0 - lam_init)
        ob = (accb[...] / lb[...]).astype(BF16)
        row = lax.broadcasted_iota(jnp.int32, (H_B, DV_B), 0)
        yb = jnp.zeros((H_B, DV_B), F32)
        for h in range(H_B):
            yb = jnp.where(row == h, _dot(ob, wuv_ref[h]), yb)
        ob_ref[...] = yb
        oc_ref[...] = accc[...] / lc[...]


def _decode(l, page_table, caches, samp, p, lam_init):
    nb, n_pages = page_table.shape
    n = PAGES_PER_STEP
    n_steps = n_pages // n

    def page_spec(a, i):
        blk = (None, None) + a.shape[2:]
        nd = a.ndim - 2
        return pl.BlockSpec(blk, lambda b, s, pt: (l, pt[b, n_pages - 1 - (s * n + i)]) + (0,) * nd)

    def samp_spec(a):
        return pl.BlockSpec((None,) + a.shape[1:], lambda b, s, pt: (b,) + (0,) * (a.ndim - 1))

    def layer_spec(a):
        return pl.BlockSpec((None,) + a.shape[1:], lambda b, s, pt: (l,) + (0,) * (a.ndim - 1))

    in_specs, args = [], []
    for a in caches:
        for i in range(n):
            in_specs.append(page_spec(a, i))
            args.append(a)
    for a in samp:
        in_specs.append(samp_spec(a))
        args.append(a)
    for a in (p["diff_lambda"], p["diff_subln"], p["w_uv_h"]):
        in_specs.append(layer_spec(a))
        args.append(a)
    out_shape = [jax.ShapeDtypeStruct((nb, H_A, 2 * DA), F32), jax.ShapeDtypeStruct((nb, H_B, DV_B), F32),
                 jax.ShapeDtypeStruct((nb, H_C, DC), F32)]
    out_specs = [pl.BlockSpec((None,) + s.shape[1:], lambda b, s_, pt: (b, 0, 0)) for s in out_shape]
    col = pltpu.VMEM((8, 1), F32)
    scratch = [col, col, pltpu.VMEM((8, 2 * DA), F32), col, col, pltpu.VMEM((8, KV_RANK_B), F32),
               col, col, pltpu.VMEM((8, DC), F32), col]
    grid_spec = pltpu.PrefetchScalarGridSpec(
        num_scalar_prefetch=1, grid=(nb, n_steps), in_specs=in_specs, out_specs=out_specs, scratch_shapes=scratch)
    return pl.pallas_call(
        functools.partial(_decode_body, lam_init=lam_init, n_steps=n_steps),
        grid_spec=grid_spec,
        out_shape=out_shape,
        compiler_params=_cparams(("parallel", "arbitrary")),
        name="decode_attn",
    )(page_table, *args)


def _merge_body(h_ref, g_ref, ya_ref, yb_ref, yc_ref, wgate_ref, wbr_ref, wout_ref, o_ref):
    h = h_ref[...]
    hn = _rms(h, g_ref[2:3, :]).astype(BF16)
    merged = None
    for n, y_ref in enumerate((ya_ref, yb_ref, yc_ref)):
        gate = jax.nn.sigmoid(_dot(hn, wgate_ref[:, n * D_MODEL:(n + 1) * D_MODEL]))
        term = gate * _dot(y_ref[...], wbr_ref[n])
        merged = term if merged is None else merged + term
    m = _dot(merged.astype(BF16), wout_ref[...])
    o_ref[...] = h + _rms(m, g_ref[3:4, :])


def _merge(h, norm_g, ya, yb, yc, p, l, tm):
    t = h.shape[0]

    def full(a):
        nd = a.ndim - 1
        return pl.BlockSpec((None,) + a.shape[1:], lambda i: (l,) + (0,) * nd)

    def rows(w):
        return pl.BlockSpec((tm, w), lambda i: (i, 0))

    return pl.pallas_call(
        _merge_body,
        grid=(t // tm,),
        in_specs=[rows(D_MODEL), full(norm_g), rows(BRANCH_W), rows(BRANCH_W), rows(BRANCH_W),
                  full(p["w_gate"]), full(p["w_branch"]), full(p["w_out"])],
        out_specs=rows(D_MODEL),
        out_shape=jax.ShapeDtypeStruct((t, D_MODEL), F32),
        compiler_params=_cparams(("parallel",)),
        name="merge",
    )(h, norm_g, ya, yb, yc, p["w_gate"], p["w_branch"], p["w_out"])


def _rope_tables(pos, rot, period, theta):
    half = rot // 2
    inv = 1.0 / (theta ** (jnp.arange(0, rot, 2, dtype=F32) / rot))
    ang = pos.astype(F32)[:, None] * inv[None, :]
    cos, sin = jnp.cos(ang), jnp.sin(ang)
    n = pos.shape[0]
    pad = jnp.zeros((n, period - rot), F32)
    c = jnp.concatenate([cos, cos, jnp.ones((n, period - rot), F32)], axis=1)
    s1 = jnp.concatenate([-sin, jnp.zeros((n, half), F32), pad], axis=1)
    s2 = jnp.concatenate([jnp.zeros((n, half), F32), sin, pad], axis=1)
    rep = LANES // period
    return jnp.stack([jnp.tile(c, (1, rep)), jnp.tile(s1, (1, rep)), jnp.tile(s2, (1, rep))])


def _prep_params(w_in, b_forget, diff_lambda, diff_subln, mla_q_norm, mla_kv_norm, mla_w_uq, mla_w_uk, mla_w_uv,
                 w_branch, w_out):
    depth = w_in.shape[0]
    widths = (H_A * 2 * DA, KV_A * 2 * DA, KV_A * 2 * DA, Q_RANK_B, KV_RANK_B, DR_B, H_C * DC, KV_C * DC, KV_C * DC,
              H_C, N_BRANCH * D_MODEL)
    offs = np.cumsum((0,) + widths)
    seg = {n: w_in[:, :, offs[i]:offs[i + 1]] for i, n in enumerate(
        ("a_q", "a_k", "a_v", "b_cq", "b_ckv", "b_kr", "c_q", "c_k", "c_v", "c_f", "gate"))}
    zeros = jnp.zeros((depth, D_MODEL, LANES - H_C), w_in.dtype)
    w_main = jnp.concatenate(
        [seg["a_q"], seg["a_k"], seg["a_v"], seg["b_cq"], seg["b_ckv"], seg["c_k"], seg["c_q"], seg["c_v"],
         seg["b_kr"], seg["b_kr"], seg["b_kr"], seg["b_kr"], seg["c_f"], zeros], axis=2).astype(BF16)
    assert w_main.shape[2] == N_MAIN
    uq = mla_w_uq.reshape(depth, Q_RANK_B, H_B, DN_B + DR_B)
    w_uq = jnp.concatenate([uq[..., :DN_B].reshape(depth, Q_RANK_B, H_B * DN_B),
                            uq[..., DN_B:].reshape(depth, Q_RANK_B, H_B * DR_B)], axis=2).astype(BF16)
    eye = jnp.eye(H_B, dtype=w_in.dtype)
    w_uk_bd = jnp.einsum("lchn,hg->lhngc", mla_w_uk, eye).reshape(depth, H_B * DN_B, H_B * KV_RANK_B).astype(BF16)
    w_uv_bd = jnp.einsum("lchv,hg->lhcgv", mla_w_uv, eye).reshape(depth, H_B * KV_RANK_B, H_B * DV_B).astype(BF16)
    w_uv_h = jnp.transpose(mla_w_uv, (0, 2, 1, 3)).astype(BF16)
    return {
        "w_main": w_main,
        "w_ft": jnp.transpose(seg["c_f"], (0, 2, 1)).astype(BF16),
        "b_f": jnp.pad(b_forget, ((0, 0), (0, LANES - H_C)))[:, None, :],
        "b_ft": b_forget[:, :, None],
        "q_norm": mla_q_norm[:, None, :],
        "kv_norm": mla_kv_norm[:, None, :],
        "w_uq": w_uq,
        "w_uk_bd": w_uk_bd,
        "w_uv_bd": w_uv_bd,
        "w_uv_h": w_uv_h,
        "w_gate": seg["gate"].astype(BF16),
        "w_branch": w_branch.astype(BF16),
        "w_out": w_out.astype(BF16),
        "diff_lambda": diff_lambda,
        "diff_subln": diff_subln[:, None, :],
    }


def kernel(x_prompt, x_sample, cache_a_k, cache_a_v, cache_b_lat, cache_b_kr, cache_c_k, cache_c_v, cache_c_logf, page_table, norm_g, ffn_w_gate, ffn_w_up, ffn_w_down, w_in, b_forget, diff_lambda, diff_subln, mla_q_norm, mla_kv_norm, mla_w_uq, mla_w_uk, mla_w_uv, w_branch, w_out):
    batch, seq, _ = x_prompt.shape
    nb = x_sample.shape[0]
    depth = norm_g.shape[0]
    past_len = page_table.shape[1] * PAGE
    tp = batch * seq
    tm_p = 512

    p = _prep_params(w_in, b_forget, diff_lambda, diff_subln, mla_q_norm, mla_kv_norm, mla_w_uq, mla_w_uk, mla_w_uv,
                     w_branch, w_out)
    wg, wu, wd = ffn_w_gate.astype(BF16), ffn_w_up.astype(BF16), ffn_w_down.astype(BF16)
    pos_p = jnp.arange(seq, dtype=jnp.int32)
    pos_s = jnp.full((nb,), past_len, dtype=jnp.int32)
    tabs_p = (_rope_tables(pos_p, ROT_A, DA, THETA_A), _rope_tables(pos_p, DR_B, DR_B, THETA_B))
    tabs_s = (_rope_tables(pos_s, ROT_A, DA, THETA_A), _rope_tables(pos_s, DR_B, DR_B, THETA_B))
    caches = (cache_a_k, cache_a_v, cache_b_lat,
              jnp.transpose(cache_b_kr, (0, 1, 3, 2)),
              jnp.transpose(cache_c_k, (0, 1, 3, 4, 2)),
              jnp.transpose(cache_c_v, (0, 1, 3, 4, 2)),
              jnp.transpose(cache_c_logf, (0, 1, 3, 2)))

    xp = x_prompt.reshape(tp, D_MODEL)
    xs = x_sample.reshape(nb, D_MODEL)
    rows_p = [[] for _ in range(7)]
    rows_s = [[] for _ in range(7)]
    for l in range(depth):
        lam_init = 0.8 - 0.6 * math.exp(-0.3 * l)
        hp = _ffn(xp, norm_g, wg, wu, wd, l, 0, tm_p)
        (ak, av, ckv, kr, ck, cv, lf, lft, aqb, akb, avb, qb, kb, cqb, ckb, cvb) = _inproj(
            hp, norm_g, p, l, tabs_p[0], tabs_p[1], tm_p)
        cum_r, cum_t = _cumsum(lf, lft, batch, seq)
        lay = lambda a: pl.BlockSpec((None,) + a.shape[1:], lambda b, i: (l,) + (0,) * (a.ndim - 1))
        ya = _flash_call(functools.partial(_flash_a_body, lam_init=lam_init), "flash_a", batch, seq, aqb, (akb, avb),
                         (p["diff_lambda"], p["diff_subln"]), [lay(p["diff_lambda"]), lay(p["diff_subln"])])
        yb = _flash_call(_flash_b_body, "flash_b", batch, seq, qb, (kb,), (p["w_uv_bd"],), [lay(p["w_uv_bd"])])
        nq = seq // TQ
        yc = _flash_call(_flash_c_body, "flash_c", batch, seq, cqb, (ckb, cvb), (cum_r, cum_t),
                         [pl.BlockSpec((TQ, H_C), lambda b, i: (b * nq + i, 0)),
                          pl.BlockSpec((H_C, seq), lambda b, i: (0, b))])
        h2 = _merge(hp, norm_g, ya, yb, yc, p, l, tm_p)
        xp = _ffn(h2, norm_g, wg, wu, wd, l, 1, tm_p)
        for dst, a in zip(rows_p, (ak.reshape(batch, seq, KV_A, 2 * DA), av.reshape(batch, seq, KV_A, 2 * DA),
                                   ckv.reshape(batch, seq, KV_RANK_B), kr.reshape(batch, seq, DR_B),
                                   ck.reshape(batch, seq, KV_C, DC), cv.reshape(batch, seq, KV_C, DC),
                                   lf.reshape(batch, seq, H_C))):
            dst.append(a)

        hs = _ffn(xs, norm_g, wg, wu, wd, l, 0, nb)
        (ak, av, ckv, kr, ck, cv, lf, lft, aqb, akb, avb, qb, kb, cqb, ckb, cvb) = _inproj(
            hs, norm_g, p, l, tabs_s[0], tabs_s[1], nb)
        eye2 = jnp.eye(2, dtype=BF16)
        qa = (aqb.reshape(nb, H_A, 1, 2, DA) * eye2[None, None, :, :, None]).reshape(nb, 2 * H_A, 2 * DA)
        rep_a = 2 * H_A // KV_A
        ka = jnp.repeat(akb.reshape(nb, KV_A, 2 * DA), rep_a, axis=1)
        va = jnp.repeat(av.reshape(nb, KV_A, 2 * DA), rep_a, axis=1)
        qb3 = qb.reshape(nb, H_B, 256)
        ql = qb3[:, :, :KV_RANK_B]
        qr = qb3[:, :, KV_RANK_B:].reshape(nb, H_B, LANES // DR_B, DR_B).sum(axis=2)
        kl = kb[:, None, :KV_RANK_B]
        krn = kb[:, None, KV_RANK_B:KV_RANK_B + DR_B]
        qc = cqb.reshape(nb, H_C, DC)
        kc = jnp.repeat(ckb.reshape(nb, KV_C, DC), H_C // KV_C, axis=1)
        vc = jnp.repeat(cv.reshape(nb, KV_C, DC), H_C // KV_C, axis=1)
        lfn = lf.reshape(nb, H_C, 1)
        oa, ob, oc = _decode(l, page_table, caches, (qa, ka, va, ql, qr, kl, krn, qc, kc, vc, lfn), p, lam_init)
        h2 = _merge(hs, norm_g, oa.reshape(nb, BRANCH_W).astype(BF16), ob.reshape(nb, BRANCH_W).astype(BF16),
                    oc.reshape(nb, BRANCH_W).astype(BF16), p, l, nb)
        xs = _ffn(h2, norm_g, wg, wu, wd, l, 1, nb)
        for dst, a in zip(rows_s, (ak.reshape(nb, 1, KV_A, 2 * DA), av.reshape(nb, 1, KV_A, 2 * DA),
                                   ckv.reshape(nb, 1, KV_RANK_B), kr.reshape(nb, 1, DR_B),
                                   ck.reshape(nb, 1, KV_C, DC), cv.reshape(nb, 1, KV_C, DC),
                                   lf.reshape(nb, 1, H_C))):
            dst.append(a)

    out = [xp.reshape(batch, seq, D_MODEL), xs.reshape(nb, 1, D_MODEL)]
    for rp, rs in zip(rows_p, rows_s):
        out.append(jnp.stack(rp))
        out.append(jnp.stack(rs))
    return tuple(out)
```

```python
import functools
import math

import numpy as np
import jax
import jax.numpy as jnp
from jax import lax
from jax.experimental import pallas as pl
from jax.experimental.pallas import tpu as pltpu

F32 = jnp.float32
BF16 = jnp.bfloat16

D_MODEL = 1024
D_FF = 2816
EPS = 1e-6
PAGE = 128
H_A, KV_A, DA, ROT_A, THETA_A = 4, 2, 64, 16, 500000.0
H_B, Q_RANK_B, KV_RANK_B, DN_B, DR_B, DV_B, THETA_B = 8, 256, 128, 64, 32, 64, 10000.0
H_C, KV_C, DC = 8, 2, 64
BRANCH_W = 512
N_BRANCH = 3
LANES = 128
NEG = -1e30
LOG2E = math.log2(math.e)

O_AQ, O_AK, O_AV, O_BCQ, O_CKV, O_CK, O_CQ, O_CV, O_KR, O_CF, N_MAIN = (
    0, 512, 768, 1024, 1280, 1408, 1536, 2048, 2176, 2304, 2432)
AUG_K0 = DC
AUG_Q0 = DC + 3 * (H_C // KV_C)

FF_CHUNK = 1408
TQ = 256
HEADS_PER_PASS = 4
VMEM_LIMIT = 56 * 1024 * 1024
PAGES_PER_STEP = 8


def _cparams(sem):
    return pltpu.CompilerParams(dimension_semantics=sem, vmem_limit_bytes=VMEM_LIMIT)


def _rms(x, g):
    return x * lax.rsqrt(jnp.mean(x * x, axis=-1, keepdims=True) + EPS) * g


def _log_sigmoid(x):
    return jnp.minimum(x, 0.0) - jnp.log(1.0 + jnp.exp(-jnp.abs(x)))


def _dot(a, b):
    return jnp.dot(a, b, preferred_element_type=F32)


def _dot_nt(a, b):
    return lax.dot_general(a, b, (((1,), (1,)), ((), ())), preferred_element_type=F32)


def _ffn_body(x_ref, g_ref, wg_ref, wu_ref, wd_ref, o_ref, *, pre, post):
    x = x_ref[...]
    xn = _rms(x, g_ref[pre:pre + 1, :]).astype(BF16)
    acc = None
    for c in range(D_FF // FF_CHUNK):
        sl = slice(c * FF_CHUNK, (c + 1) * FF_CHUNK)
        g = _dot(xn, wg_ref[:, sl])
        u = _dot(xn, wu_ref[:, sl])
        h = (g * jax.nn.sigmoid(g) * u).astype(BF16)
        d = _dot(h, wd_ref[sl, :])
        acc = d if acc is None else acc + d
    o_ref[...] = x + 0.5 * _rms(acc, g_ref[post:post + 1, :])


def _ffn(x, norm_g, wg, wu, wd, l, j, tm):
    t = x.shape[0]
    once = pl.Buffered(1)
    return pl.pallas_call(
        functools.partial(_ffn_body, pre=3 * j + (0 if j == 0 else 1), post=3 * j + (1 if j == 0 else 2)),
        grid=(t // tm,),
        in_specs=[
            pl.BlockSpec((tm, D_MODEL), lambda i: (i, 0)),
            pl.BlockSpec((None, 6, D_MODEL), lambda i: (l, 0, 0)),
            pl.BlockSpec((None, None, D_MODEL, D_FF), lambda i: (l, j, 0, 0), pipeline_mode=once),
            pl.BlockSpec((None, None, D_MODEL, D_FF), lambda i: (l, j, 0, 0), pipeline_mode=once),
            pl.BlockSpec((None, None, D_FF, D_MODEL), lambda i: (l, j, 0, 0), pipeline_mode=once),
        ],
        out_specs=pl.BlockSpec((tm, D_MODEL), lambda i: (i, 0)),
        out_shape=jax.ShapeDtypeStruct((t, D_MODEL), F32),
        compiler_params=_cparams(("parallel",)),
        name="ffn",
    )(x, norm_g, wg, wu, wd)


def _rope(x, tab_ref, half):
    return (x * tab_ref[0]
            + pltpu.roll(x, LANES - half, 1) * tab_ref[1]
            + pltpu.roll(x, half, 1) * tab_ref[2])


def _split3(x):
    hi = x.astype(BF16)
    r1 = x - hi.astype(F32)
    mid = r1.astype(BF16)
    lo = (r1 - mid.astype(F32)).astype(BF16)
    return hi, mid, lo


def _inproj_body(h_ref, g_ref, wm_ref, wt_ref, bf_ref, qn_ref, kvn_ref, kvnc_ref, wuq_ref, wuk_ref,
                 ta_ref, tb_ref, sk_ref, sq_ref, ck1_ref, cq1_ref,
                 ak_ref, av_ref, ckv_ref, kr_ref, ck_ref, cv_ref, lf_ref,
                 qa_ref, akb_ref, avt_ref, qb_ref, kb_ref, ckvt_ref, qc_ref, kc_ref, cvt_ref,
                 carry, *, seq_tiles):
    tm = h_ref.shape[0]
    hn = _rms(h_ref[...], g_ref[2:3, :]).astype(BF16)
    z = _dot(hn, wm_ref[...])
    zt = _dot_nt(wt_ref[...], hn)
    lane = lax.broadcasted_iota(jnp.int32, (1, LANES), 1)
    low = lane < DC

    def slab(off, i=0):
        return z[:, off + i * LANES: off + (i + 1) * LANES]

    for h in range(H_A):
        q = _rope(slab(O_AQ, h), ta_ref, ROT_A // 2) * (DA ** -0.5 * LOG2E)
        qa_ref[:, (2 * h) * LANES:(2 * h + 1) * LANES] = jnp.where(low, q, 0.0).astype(BF16)
        qa_ref[:, (2 * h + 1) * LANES:(2 * h + 2) * LANES] = jnp.where(low, 0.0, q).astype(BF16)
    for i in range(2):
        k = _rope(slab(O_AK, i), ta_ref, ROT_A // 2)
        ak_ref[:, i * LANES:(i + 1) * LANES] = k
        akb_ref[:, i * LANES:(i + 1) * LANES] = k.astype(BF16)
    av_ref[...] = z[:, O_AV:O_AV + 2 * LANES]
    avt_ref[...] = zt[0:256].astype(BF16)

    qc = _rms(z[:, O_BCQ:O_BCQ + Q_RANK_B], qn_ref[...]).astype(BF16)
    qb = _dot(qc, wuq_ref[...])
    q_lat = _dot(qb[:, :H_B * DN_B].astype(BF16), wuk_ref[...])
    scale_b = (DN_B + DR_B) ** -0.5 * LOG2E
    for s in range(2):
        qr = _rope(qb[:, H_B * DN_B + s * LANES: H_B * DN_B + (s + 1) * LANES], tb_ref, DR_B // 2) * scale_b
        for hh in range(4):
            h = s * 4 + hh
            qb_ref[:, h * 256: h * 256 + LANES] = (q_lat[:, h * LANES:(h + 1) * LANES] * scale_b).astype(BF16)
            sel = (lane >= hh * DR_B) & (lane < (hh + 1) * DR_B)
            qb_ref[:, h * 256 + LANES: (h + 1) * 256] = jnp.where(sel, qr, 0.0).astype(BF16)
    ckv = _rms(slab(O_CKV), kvn_ref[...])
    ckv_ref[...] = ckv
    kr = _rope(slab(O_KR), tb_ref, DR_B // 2)
    kr_ref[...] = kr[:, :DR_B]
    kb_ref[:, :LANES] = ckv.astype(BF16)
    kb_ref[:, LANES:] = kr.astype(BF16)
    xt = zt[384:512]
    ckvt_ref[...] = (xt * lax.rsqrt(jnp.mean(xt * xt, axis=0, keepdims=True) + EPS) * kvnc_ref[...]).astype(BF16)

    ck = slab(O_CK)
    ck_ref[...] = ck
    cv_ref[...] = slab(O_CV)
    cvt_ref[...] = zt[256:384].astype(BF16)
    lf = _log_sigmoid(slab(O_CF) + bf_ref[...])
    lf_ref[...] = lf[:, :H_C]

    @pl.when(pl.program_id(0) % seq_tiles == 0)
    def _():
        carry[...] = jnp.zeros_like(carry)

    row = lax.broadcasted_iota(jnp.int32, (tm, LANES), 0)
    cum = lf
    s = 1
    while s < tm:
        cum = cum + jnp.where(row >= s, pltpu.roll(cum, s, 0), 0.0)
        s *= 2
    cum = cum + carry[...]
    carry[...] = cum[tm - 1:tm, :]
    parts = jnp.concatenate(_split3(cum * LOG2E), axis=1)
    kaug = _dot(parts, sk_ref[...]) + ck1_ref[...]
    qaug = _dot(parts, sq_ref[...]) + cq1_ref[...]
    for g in range(KV_C):
        kg = ck if g == 0 else pltpu.roll(ck, LANES - DC, 1)
        kc_ref[:, g * LANES:(g + 1) * LANES] = (jnp.where(low, kg, 0.0) + kaug[:, g * LANES:(g + 1) * LANES]).astype(BF16)
    for h in range(H_C):
        q = slab(O_CQ, h // 2)
        q = q if h % 2 == 0 else pltpu.roll(q, LANES - DC, 1)
        qc_ref[:, h * LANES:(h + 1) * LANES] = (jnp.where(low, q * (DC ** -0.5 * LOG2E), 0.0)
                                                + qaug[:, h * LANES:(h + 1) * LANES]).astype(BF16)


def _inproj(h, norm_g, p, l, tab_a, tab_b, tm, seq_tiles):
    t = h.shape[0]
    n_pos = tab_a.shape[1] // tm

    def full(a):
        nd = a.ndim - 1
        return pl.BlockSpec((None,) + a.shape[1:], lambda i: (l,) + (0,) * nd)

    def const(a):
        return pl.BlockSpec(a.shape, lambda i: (0,) * a.ndim)

    def rows(w):
        return pl.BlockSpec((tm, w), lambda i: (i, 0))

    def cols(w):
        return pl.BlockSpec((w, tm), lambda i: (0, i))

    tab_spec = pl.BlockSpec((3, tm, LANES), lambda i: (0, i % n_pos, 0))
    f32_w = (256, 256, 128, DR_B, 128, 128, H_C)
    bf_w = ((1024, 0), (256, 0), (256, 1), (H_B * 256, 0), (256, 0), (128, 1), (1024, 0), (256, 0), (128, 1))
    out_shape = ([jax.ShapeDtypeStruct((t, w), F32) for w in f32_w]
                 + [jax.ShapeDtypeStruct((w, t) if tr else (t, w), BF16) for w, tr in bf_w])
    out_specs = [rows(w) for w in f32_w] + [cols(w) if tr else rows(w) for w, tr in bf_w]
    names = ("w_main", "w_t", "b_f", "q_norm", "kv_norm", "kv_norm_col", "w_uq", "w_uk_bd")
    consts = (p["sel_k"], p["sel_q"], p["one_k"], p["one_q"])
    return pl.pallas_call(
        functools.partial(_inproj_body, seq_tiles=seq_tiles),
        grid=(t // tm,),
        in_specs=([rows(D_MODEL), full(norm_g)] + [full(p[n]) for n in names] + [tab_spec, tab_spec]
                  + [const(a) for a in consts]),
        out_specs=out_specs,
        out_shape=out_shape,
        scratch_shapes=[pltpu.VMEM((1, LANES), F32)],
        compiler_params=_cparams(("arbitrary",)),
        name="inproj",
    )(h, norm_g, *[p[n] for n in names], tab_a, tab_b, *consts)


def _flash_pass(streams, k_ref, vt_ref, qi):
    def scores(j, q_stack, kcols):
        r0 = pl.multiple_of(j * TQ, TQ)
        return _dot_nt(k_ref[pl.ds(r0, TQ), kcols], q_stack)

    def update(j, carry, diagonal, vrows):
        m, l, acc, st = carry
        if diagonal:
            key = lax.broadcasted_iota(jnp.int32, st.shape, 0)
            qry = lax.broadcasted_iota(jnp.int32, st.shape, 1) & (TQ - 1)
            st = jnp.where(key <= qry, st, NEG)
        m_new = jnp.maximum(m, jnp.max(st, axis=0, keepdims=True))
        alpha = jnp.exp2(m - m_new)
        pt = jnp.exp2(st - m_new)
        l = alpha * l + jnp.sum(pt, axis=0, keepdims=True)
        r0 = pl.multiple_of(j * TQ, TQ)
        acc = alpha * acc + _dot(vt_ref[vrows, pl.ds(r0, TQ)], pt.astype(BF16))
        return m_new, l, acc

    def block(j, carries, diagonal):
        return tuple(update(j, c + (scores(j, s[0], s[1]),), diagonal, s[2]) for c, s in zip(carries, streams))

    def init(q_stack, kcols, vrows):
        w = q_stack.shape[0]
        return (jnp.full((1, w), NEG, F32), jnp.zeros((1, w), F32), jnp.zeros((vrows.stop - vrows.start, w), F32))

    carries = lax.fori_loop(0, qi, lambda j, c: block(j, c, False), tuple(init(*s) for s in streams))
    return [acc * (1.0 / l) for m, l, acc in block(qi, carries, True)]


def _flash_heads(q_ref, width, n_heads, kcols, vrows, k_ref, vt_ref, qi):
    streams = []
    for first in range(0, n_heads, HEADS_PER_PASS):
        slabs = [q_ref[:, (first + v) * width:(first + v + 1) * width] for v in range(HEADS_PER_PASS)]
        streams.append((slabs[0] if len(slabs) == 1 else jnp.concatenate(slabs, axis=0), kcols(first), vrows(first)))
    outs = _flash_pass(streams, k_ref, vt_ref, qi)
    return [ot[:, v * TQ:(v + 1) * TQ] for ot in outs for v in range(HEADS_PER_PASS)]


def _to_rows(yt):
    eye = (lax.broadcasted_iota(jnp.int32, (TQ, TQ), 0) == lax.broadcasted_iota(jnp.int32, (TQ, TQ), 1))
    return _dot_nt(jnp.where(eye, 1.0, 0.0).astype(BF16), yt).astype(BF16)


def _flash_a_body(q_ref, k_ref, vt_ref, dl_ref, sub_ref, o_ref, *, lam_init):
    qi = pl.program_id(1)
    dl = dl_ref[...]
    lam = (jnp.exp(jnp.sum(dl[0:1] * dl[1:2], axis=-1, keepdims=True))
           - jnp.exp(jnp.sum(dl[2:3] * dl[3:4], axis=-1, keepdims=True)) + lam_init)
    def group(slab):
        g = slab // (2 * H_A // KV_A)
        return slice(g * 2 * DA, (g + 1) * 2 * DA)

    outs = _flash_heads(q_ref, LANES, 2 * H_A, group, group, k_ref, vt_ref, qi)
    for h in range(H_A):
        y = outs[2 * h] - lam * outs[2 * h + 1]
        y = y * lax.rsqrt(jnp.mean(y * y, axis=0, keepdims=True) + EPS) * sub_ref[...] * (1.0 - lam_init)
        o_ref[:, h * 2 * DA:(h + 1) * 2 * DA] = _to_rows(y.astype(BF16))


def _flash_b_body(q_ref, k_ref, vt_ref, wuvt_ref, o_ref):
    qi = pl.program_id(1)
    outs = _flash_heads(q_ref, 256, H_B, lambda h: slice(0, 256), lambda h: slice(0, KV_RANK_B), k_ref, vt_ref, qi)
    ybt = _dot(wuvt_ref[...], jnp.concatenate([o.astype(BF16) for o in outs], axis=0))
    o_ref[...] = _to_rows(ybt.astype(BF16))


def _flash_c_body(q_ref, k_ref, vt_ref, o_ref):
    qi = pl.program_id(1)
    per = H_C // KV_C
    outs = _flash_heads(q_ref, LANES, H_C, lambda h: slice((h // per) * LANES, (h // per + 1) * LANES),
                        lambda h: slice((h // per) * DC, (h // per + 1) * DC), k_ref, vt_ref, qi)
    o_ref[...] = _to_rows(jnp.concatenate([o.astype(BF16) for o in outs], axis=0))


def _flash_call(body, name, batch, seq, q, k, vt, extra, extra_specs):
    t = q.shape[0]
    nq = seq // TQ
    return pl.pallas_call(
        body,
        grid=(batch, nq),
        in_specs=[pl.BlockSpec((TQ, q.shape[1]), lambda b, i: (b * nq + i, 0)),
                  pl.BlockSpec((seq, k.shape[1]), lambda b, i: (b, 0)),
                  pl.BlockSpec((vt.shape[0], seq), lambda b, i: (0, b))] + extra_specs,
        out_specs=pl.BlockSpec((TQ, BRANCH_W), lambda b, i: (b * nq + i, 0)),
        out_shape=jax.ShapeDtypeStruct((t, BRANCH_W), BF16),
        compiler_params=_cparams(("parallel", "arbitrary")),
        name=name,
    )(q, k, vt, *extra)


def _decode_body(pt_ref, *refs, lam_init, n_steps):
    n = PAGES_PER_STEP
    ak_p, av_p, lat_p, kr_p, ck_p, cv_p, lf_p = (refs[i * n:(i + 1) * n] for i in range(7))
    (qa_ref, ka_ref, va_ref, ql_ref, qr_ref, kl_ref, krn_ref, qc_ref, kc_ref, vc_ref, lfn_ref,
     dl_ref, sub_ref, wuv_ref) = refs[7 * n:7 * n + 14]
    oa_ref, ob_ref, oc_ref = refs[7 * n + 14:7 * n + 17]
    ma, la, acca, mb, lb, accb, mc, lc, accc, carry = refs[7 * n + 17:]
    step = pl.program_id(1)

    qa = qa_ref[...]
    ql = ql_ref[...]
    qr = qr_ref[...]
    qc = qc_ref[...]

    @pl.when(step == 0)
    def _():
        ma[...] = jnp.sum(qa * ka_ref[...], axis=-1, keepdims=True)
        la[...] = jnp.ones_like(la)
        acca[...] = va_ref[...]
        mb[...] = (jnp.sum(ql * kl_ref[...], axis=-1, keepdims=True)
                   + jnp.sum(qr * krn_ref[...], axis=-1, keepdims=True))
        lb[...] = jnp.ones_like(lb)
        accb[...] = jnp.broadcast_to(kl_ref[...], accb.shape)
        mc[...] = jnp.sum(qc * kc_ref[...], axis=-1, keepdims=True)
        lc[...] = jnp.ones_like(lc)
        accc[...] = vc_ref[...]
        carry[...] = lfn_ref[...]

    def update(s_list, m_ref, l_ref, acc_ref, pv):
        s = jnp.concatenate(s_list, axis=1)
        m_old = m_ref[...]
        m_new = jnp.maximum(m_old, jnp.max(s, axis=-1, keepdims=True))
        alpha = jnp.exp2(m_old - m_new)
        pr = jnp.exp2(s - m_new)
        l_ref[...] = alpha * l_ref[...] + jnp.sum(pr, axis=-1, keepdims=True)
        acc = alpha * acc_ref[...]
        wk = s.shape[1] // n
        for i in range(n):
            acc = acc + pv(i, pr[:, i * wk:(i + 1) * wk])
        acc_ref[...] = acc
        m_ref[...] = m_new

    def by_group(x, shape):
        first = lax.broadcasted_iota(jnp.int32, shape, 0) < 4
        return jnp.where(first, x[0], x[1])

    def group_rows(pr, g):
        row = lax.broadcasted_iota(jnp.int32, pr.shape, 0)
        return jnp.where((row < 4) == (g == 0), pr, 0.0)

    row_a = lax.broadcasted_iota(jnp.int32, (8, 2 * PAGE), 0)
    col_a = lax.broadcasted_iota(jnp.int32, (8, 2 * PAGE), 1)
    own = (col_a & 1) == (row_a >> 2)
    s_a = [jnp.where(own, _dot_nt(qa, ak_p[i][...]), NEG) for i in range(n)]
    update(s_a, ma, la, acca, lambda i, pr: _dot(pr, av_p[i][...]))

    s_b = [_dot_nt(ql, lat_p[i][...]) + _dot(qr, kr_p[i][...]) for i in range(n)]
    update(s_b, mb, lb, accb, lambda i, pr: _dot(pr, lat_p[i][...]))

    lane = lax.broadcasted_iota(jnp.int32, (H_C, PAGE), 1)
    s_c = []
    run = carry[...]
    for i in range(n):
        x = lf_p[i][...]
        suf = x
        sh = 1
        while sh < PAGE:
            suf = suf + jnp.where(lane + sh < PAGE, pltpu.roll(suf, PAGE - sh, 1), 0.0)
            sh *= 2
        bias = ((suf - x) + run) * LOG2E
        run = run + suf[:, 0:1]
        s_c.append(by_group([_dot(qc, ck_p[i][g]) for g in range(KV_C)], (8, PAGE)) + bias)
    carry[...] = run

    def pv_c(i, pr):
        return sum(_dot_nt(group_rows(pr, g), cv_p[i][g]) for g in range(KV_C))

    update(s_c, mc, lc, accc, pv_c)

    @pl.when(step == n_steps - 1)
    def _():
        dl = dl_ref[...]
        lam = (jnp.exp(jnp.sum(dl[0:1] * dl[1:2], axis=-1, keepdims=True))
               - jnp.exp(jnp.sum(dl[2:3] * dl[3:4], axis=-1, keepdims=True)) + lam_init)
        oa = acca[...] / la[...]
        for h in range(H_A):
            y = oa[2 * h:2 * h + 1] - lam * oa[2 * h + 1:2 * h + 2]
            oa_ref[h:h + 1, :] = _rms(y, sub_ref[...]) * (1.0 - lam_init)
        ob = (accb[...] / lb[...]).astype(BF16)
        row = lax.broadcasted_iota(jnp.int32, (H_B, DV_B), 0)
        yb = jnp.zeros((H_B, DV_B), F32)
        for h in range(H_B):
            yb = jnp.where(row == h, _dot(ob, wuv_ref[h]), yb)
        ob_ref[...] = yb
        oc_ref[...] = accc[...] / lc[...]


def _decode(l, page_table, caches, samp, p, lam_init):
    nb, n_pages = page_table.shape
    n = PAGES_PER_STEP
    n_steps = n_pages // n

    def page_spec(a, i):
        blk = (None, None) + a.shape[2:]
        nd = a.ndim - 2
        return pl.BlockSpec(blk, lambda b, s, pt: (l, pt[b, n_pages - 1 - (s * n + i)]) + (0,) * nd)

    def samp_spec(a):
        return pl.BlockSpec((None,) + a.shape[1:], lambda b, s, pt: (b,) + (0,) * (a.ndim - 1))

    def layer_spec(a):
        return pl.BlockSpec((None,) + a.shape[1:], lambda b, s, pt: (l,) + (0,) * (a.ndim - 1))

    in_specs, args = [], []
    for a in caches:
        for i in range(n):
            in_specs.append(page_spec(a, i))
            args.append(a)
    for a in samp:
        in_specs.append(samp_spec(a))
        args.append(a)
    for a in (p["diff_lambda"], p["diff_subln"], p["w_uv_h"]):
        in_specs.append(layer_spec(a))
        args.append(a)
    out_shape = [jax.ShapeDtypeStruct((nb, H_A, 2 * DA), F32), jax.ShapeDtypeStruct((nb, H_B, DV_B), F32),
                 jax.ShapeDtypeStruct((nb, H_C, DC), F32)]
    out_specs = [pl.BlockSpec((None,) + s.shape[1:], lambda b, s_, pt: (b, 0, 0)) for s in out_shape]
    col = pltpu.VMEM((8, 1), F32)
    scratch = [col, col, pltpu.VMEM((8, 2 * DA), F32), col, col, pltpu.VMEM((8, KV_RANK_B), F32),
               col, col, pltpu.VMEM((8, DC), F32), col]
    grid_spec = pltpu.PrefetchScalarGridSpec(
        num_scalar_prefetch=1, grid=(nb, n_steps), in_specs=in_specs, out_specs=out_specs, scratch_shapes=scratch)
    return pl.pallas_call(
        functools.partial(_decode_body, lam_init=lam_init, n_steps=n_steps),
        grid_spec=grid_spec,
        out_shape=out_shape,
        compiler_params=_cparams(("parallel", "arbitrary")),
        name="decode_attn",
    )(page_table, *args)


def _merge_body(h_ref, g_ref, ya_ref, yb_ref, yc_ref, wgate_ref, wbr_ref, wout_ref, o_ref):
    h = h_ref[...]
    hn = _rms(h, g_ref[2:3, :]).astype(BF16)
    merged = None
    for n, y_ref in enumerate((ya_ref, yb_ref, yc_ref)):
        gate = jax.nn.sigmoid(_dot(hn, wgate_ref[:, n * D_MODEL:(n + 1) * D_MODEL]))
        term = gate * _dot(y_ref[...], wbr_ref[n])
        merged = term if merged is None else merged + term
    m = _dot(merged.astype(BF16), wout_ref[...])
    o_ref[...] = h + _rms(m, g_ref[3:4, :])


def _merge(h, norm_g, ya, yb, yc, p, l, tm):
    t = h.shape[0]

    def full(a):
        nd = a.ndim - 1
        return pl.BlockSpec((None,) + a.shape[1:], lambda i: (l,) + (0,) * nd)

    def rows(w):
        return pl.BlockSpec((tm, w), lambda i: (i, 0))

    return pl.pallas_call(
        _merge_body,
        grid=(t // tm,),
        in_specs=[rows(D_MODEL), full(norm_g), rows(BRANCH_W), rows(BRANCH_W), rows(BRANCH_W),
                  full(p["w_gate"]), full(p["w_branch"]), full(p["w_out"])],
        out_specs=rows(D_MODEL),
        out_shape=jax.ShapeDtypeStruct((t, D_MODEL), F32),
        compiler_params=_cparams(("parallel",)),
        name="merge",
    )(h, norm_g, ya, yb, yc, p["w_gate"], p["w_branch"], p["w_out"])


def _rope_tables(pos, rot, period, theta):
    half = rot // 2
    inv = 1.0 / (theta ** (jnp.arange(0, rot, 2, dtype=F32) / rot))
    ang = pos.astype(F32)[:, None] * inv[None, :]
    cos, sin = jnp.cos(ang), jnp.sin(ang)
    n = pos.shape[0]
    pad = jnp.zeros((n, period - rot), F32)
    c = jnp.concatenate([cos, cos, jnp.ones((n, period - rot), F32)], axis=1)
    s1 = jnp.concatenate([-sin, jnp.zeros((n, half), F32), pad], axis=1)
    s2 = jnp.concatenate([jnp.zeros((n, half), F32), sin, pad], axis=1)
    rep = LANES // period
    return jnp.stack([jnp.tile(c, (1, rep)), jnp.tile(s1, (1, rep)), jnp.tile(s2, (1, rep))])


def _bias_selectors():
    per = H_C // KV_C
    sel_k = np.zeros((3 * LANES, KV_C * LANES), np.float32)
    sel_q = np.zeros((3 * LANES, H_C * LANES), np.float32)
    one_k = np.zeros((1, KV_C * LANES), np.float32)
    one_q = np.zeros((1, H_C * LANES), np.float32)
    for h in range(H_C):
        g, hh = divmod(h, per)
        for part in range(3):
            sel_k[part * LANES + h, g * LANES + AUG_K0 + 3 * hh + part] = -1.0
            sel_q[part * LANES + h, h * LANES + AUG_Q0 + part] = 1.0
            one_q[0, h * LANES + AUG_K0 + 3 * hh + part] = 1.0
    for g in range(KV_C):
        for part in range(3):
            one_k[0, g * LANES + AUG_Q0 + part] = 1.0
    return jnp.asarray(sel_k, BF16), jnp.asarray(sel_q, BF16), jnp.asarray(one_k), jnp.asarray(one_q)


def _prep_params(w_in, b_forget, diff_lambda, diff_subln, mla_q_norm, mla_kv_norm, mla_w_uq, mla_w_uk, mla_w_uv,
                 w_branch, w_out):
    depth = w_in.shape[0]
    widths = (H_A * 2 * DA, KV_A * 2 * DA, KV_A * 2 * DA, Q_RANK_B, KV_RANK_B, DR_B, H_C * DC, KV_C * DC, KV_C * DC,
              H_C, N_BRANCH * D_MODEL)
    offs = np.cumsum((0,) + widths)
    seg = {n: w_in[:, :, offs[i]:offs[i + 1]] for i, n in enumerate(
        ("a_q", "a_k", "a_v", "b_cq", "b_ckv", "b_kr", "c_q", "c_k", "c_v", "c_f", "gate"))}
    zeros = jnp.zeros((depth, D_MODEL, LANES - H_C), w_in.dtype)
    w_main = jnp.concatenate(
        [seg["a_q"], seg["a_k"], seg["a_v"], seg["b_cq"], seg["b_ckv"], seg["c_k"], seg["c_q"], seg["c_v"],
         seg["b_kr"], seg["b_kr"], seg["b_kr"], seg["b_kr"], seg["c_f"], zeros], axis=2).astype(BF16)
    assert w_main.shape[2] == N_MAIN
    w_t = jnp.transpose(jnp.concatenate([seg["a_v"], seg["c_v"], seg["b_ckv"]], axis=2), (0, 2, 1)).astype(BF16)
    uq = mla_w_uq.reshape(depth, Q_RANK_B, H_B, DN_B + DR_B)
    w_uq = jnp.concatenate([uq[..., :DN_B].reshape(depth, Q_RANK_B, H_B * DN_B),
                            uq[..., DN_B:].reshape(depth, Q_RANK_B, H_B * DR_B)], axis=2).astype(BF16)
    eye = jnp.eye(H_B, dtype=w_in.dtype)
    w_uk_bd = jnp.einsum("lchn,hg->lhngc", mla_w_uk, eye).reshape(depth, H_B * DN_B, H_B * KV_RANK_B).astype(BF16)
    w_uv_bdt = jnp.einsum("lchv,hg->lgvhc", mla_w_uv, eye).reshape(depth, H_B * DV_B, H_B * KV_RANK_B).astype(BF16)
    w_uv_h = jnp.transpose(mla_w_uv, (0, 2, 1, 3)).astype(BF16)
    sel_k, sel_q, one_k, one_q = _bias_selectors()
    return {
        "w_main": w_main,
        "w_t": w_t,
        "b_f": jnp.pad(b_forget, ((0, 0), (0, LANES - H_C)))[:, None, :],
        "q_norm": mla_q_norm[:, None, :],
        "kv_norm": mla_kv_norm[:, None, :],
        "kv_norm_col": mla_kv_norm[:, :, None],
        "w_uq": w_uq,
        "w_uk_bd": w_uk_bd,
        "w_uv_bdt": w_uv_bdt,
        "w_uv_h": w_uv_h,
        "w_gate": seg["gate"].astype(BF16),
        "w_branch": w_branch.astype(BF16),
        "w_out": w_out.astype(BF16),
        "diff_lambda": diff_lambda,
        "diff_subln": diff_subln[:, None, :],
        "diff_subln_col": diff_subln[:, :, None],
        "sel_k": sel_k, "sel_q": sel_q, "one_k": one_k, "one_q": one_q,
    }


def kernel(x_prompt, x_sample, cache_a_k, cache_a_v, cache_b_lat, cache_b_kr, cache_c_k, cache_c_v, cache_c_logf, page_table, norm_g, ffn_w_gate, ffn_w_up, ffn_w_down, w_in, b_forget, diff_lambda, diff_subln, mla_q_norm, mla_kv_norm, mla_w_uq, mla_w_uk, mla_w_uv, w_branch, w_out):
    batch, seq, _ = x_prompt.shape
    nb = x_sample.shape[0]
    depth = norm_g.shape[0]
    past_len = page_table.shape[1] * PAGE
    tp = batch * seq
    tm_p = 512

    p = _prep_params(w_in, b_forget, diff_lambda, diff_subln, mla_q_norm, mla_kv_norm, mla_w_uq, mla_w_uk, mla_w_uv,
                     w_branch, w_out)
    wg, wu, wd = ffn_w_gate.astype(BF16), ffn_w_up.astype(BF16), ffn_w_down.astype(BF16)
    pos_p = jnp.arange(seq, dtype=jnp.int32)
    pos_s = jnp.full((nb,), past_len, dtype=jnp.int32)
    tabs_p = (_rope_tables(pos_p, ROT_A, DA, THETA_A), _rope_tables(pos_p, DR_B, DR_B, THETA_B))
    tabs_s = (_rope_tables(pos_s, ROT_A, DA, THETA_A), _rope_tables(pos_s, DR_B, DR_B, THETA_B))
    n_pool = cache_a_k.shape[1]
    caches = (cache_a_k.reshape(depth, n_pool, KV_A * PAGE, 2 * DA),
              cache_a_v.reshape(depth, n_pool, KV_A * PAGE, 2 * DA),
              cache_b_lat,
              jnp.transpose(cache_b_kr, (0, 1, 3, 2)),
              jnp.transpose(cache_c_k, (0, 1, 3, 4, 2)),
              jnp.transpose(cache_c_v, (0, 1, 3, 4, 2)),
              jnp.transpose(cache_c_logf, (0, 1, 3, 2)))

    xp = x_prompt.reshape(tp, D_MODEL)
    xs = x_sample.reshape(nb, D_MODEL)
    rows_p = [[] for _ in range(7)]
    rows_s = [[] for _ in range(7)]
    for l in range(depth):
        lam_init = 0.8 - 0.6 * math.exp(-0.3 * l)
        lay = lambda a: pl.BlockSpec((None,) + a.shape[1:], lambda b, i: (l,) + (0,) * (a.ndim - 1))
        hp = _ffn(xp, norm_g, wg, wu, wd, l, 0, tm_p)
        (ak, av, ckv, kr, ck, cv, lf, qa, akb, avt, qb, kb, ckvt, qc, kc, cvt) = _inproj(
            hp, norm_g, p, l, tabs_p[0], tabs_p[1], tm_p, seq // tm_p)
        ya = _flash_call(functools.partial(_flash_a_body, lam_init=lam_init), "flash_a", batch, seq, qa, akb, avt,
                         (p["diff_lambda"], p["diff_subln_col"]), [lay(p["diff_lambda"]), lay(p["diff_subln_col"])])
        yb = _flash_call(_flash_b_body, "flash_b", batch, seq, qb, kb, ckvt, (p["w_uv_bdt"],), [lay(p["w_uv_bdt"])])
        yc = _flash_call(_flash_c_body, "flash_c", batch, seq, qc, kc, cvt, (), [])
        h2 = _merge(hp, norm_g, ya, yb, yc, p, l, tm_p)
        xp = _ffn(h2, norm_g, wg, wu, wd, l, 1, tm_p)
        for dst, a in zip(rows_p, (ak.reshape(batch, seq, KV_A, 2 * DA), av.reshape(batch, seq, KV_A, 2 * DA),
                                   ckv.reshape(batch, seq, KV_RANK_B), kr.reshape(batch, seq, DR_B),
                                   ck.reshape(batch, seq, KV_C, DC), cv.reshape(batch, seq, KV_C, DC),
                                   lf.reshape(batch, seq, H_C))):
            dst.append(a)

        hs = _ffn(xs, norm_g, wg, wu, wd, l, 0, nb)
        (ak, av, ckv, kr, ck, cv, lf, qa, akb, avt, qb, kb, ckvt, qc, kc, cvt) = _inproj(
            hs, norm_g, p, l, tabs_s[0], tabs_s[1], nb, 1)
        rep_a = 2 * H_A // KV_A
        qa3 = qa.astype(F32).reshape(nb, 2 * H_A, 2 * DA)
        ka = jnp.repeat(akb.astype(F32).reshape(nb, KV_A, 2 * DA), rep_a, axis=1)
        va = jnp.repeat(av.reshape(nb, KV_A, 2 * DA), rep_a, axis=1)
        qb3 = qb.astype(F32).reshape(nb, H_B, 256)
        ql = qb3[:, :, :KV_RANK_B]
        qr = qb3[:, :, KV_RANK_B:].reshape(nb, H_B, LANES // DR_B, DR_B).sum(axis=2)
        kbf = kb.astype(F32)
        kl = kbf[:, None, :KV_RANK_B]
        krn = kbf[:, None, KV_RANK_B:KV_RANK_B + DR_B]
        qc3 = qc.astype(F32).reshape(nb, H_C, LANES)[:, :, :DC]
        kc3 = jnp.repeat(ck.astype(BF16).astype(F32).reshape(nb, KV_C, DC), H_C // KV_C, axis=1)
        vc3 = jnp.repeat(cv.reshape(nb, KV_C, DC), H_C // KV_C, axis=1)
        lfn = lf.reshape(nb, H_C, 1)
        oa, ob, oc = _decode(l, page_table, caches, (qa3, ka, va, ql, qr, kl, krn, qc3, kc3, vc3, lfn), p, lam_init)
        h2 = _merge(hs, norm_g, oa.reshape(nb, BRANCH_W).astype(BF16), ob.reshape(nb, BRANCH_W).astype(BF16),
                    oc.reshape(nb, BRANCH_W).astype(BF16), p, l, nb)
        xs = _ffn(h2, norm_g, wg, wu, wd, l, 1, nb)
        for dst, a in zip(rows_s, (ak.reshape(nb, 1, KV_A, 2 * DA), av.reshape(nb, 1, KV_A, 2 * DA),
                                   ckv.reshape(nb, 1, KV_RANK_B), kr.reshape(nb, 1, DR_B),
                                   ck.reshape(nb, 1, KV_C, DC), cv.reshape(nb, 1, KV_C, DC),
                                   lf.reshape(nb, 1, H_C))):
            dst.append(a)

    out = [xp.reshape(batch, seq, D_MODEL), xs.reshape(nb, 1, D_MODEL)]
    for rp, rs in zip(rows_p, rows_s):
        out.append(jnp.stack(rp))
        out.append(jnp.stack(rs))
    return tuple(out)
```

```python
import functools
import math

import numpy as np
import jax
import jax.numpy as jnp
from jax import lax
from jax.experimental import pallas as pl
from jax.experimental.pallas import tpu as pltpu

F32 = jnp.float32
BF16 = jnp.bfloat16

D_MODEL = 1024
D_FF = 2816
EPS = 1e-6
PAGE = 128
H_A, KV_A, DA, ROT_A, THETA_A = 4, 2, 64, 16, 500000.0
H_B, Q_RANK_B, KV_RANK_B, DN_B, DR_B, DV_B, THETA_B = 8, 256, 128, 64, 32, 64, 10000.0
H_C, KV_C, DC = 8, 2, 64
BRANCH_W = 512
N_BRANCH = 3
LANES = 128
NEG = -1e30
LOG2E = math.log2(math.e)

O_AQ, O_AK, O_AV, O_BCQ, O_CKV, O_CK, O_CQ, O_CV, O_KR, O_CF, N_MAIN = (
    0, 512, 768, 1024, 1280, 1408, 1536, 2048, 2176, 2304, 2432)
AUG_K0 = DC
AUG_Q0 = DC + 3 * (H_C // KV_C)

FF_CHUNK = 1408
TQ = 256
HEADS_PER_PASS = 4
VMEM_LIMIT = 56 * 1024 * 1024
PAGES_PER_STEP = 16
N_CACHES = 7


def _cparams(sem):
    return pltpu.CompilerParams(dimension_semantics=sem, vmem_limit_bytes=VMEM_LIMIT)


def _rms(x, g):
    return x * lax.rsqrt(jnp.mean(x * x, axis=-1, keepdims=True) + EPS) * g


def _log_sigmoid(x):
    return jnp.minimum(x, 0.0) - jnp.log(1.0 + jnp.exp(-jnp.abs(x)))


def _dot(a, b):
    return jnp.dot(a, b, preferred_element_type=F32)


def _dot_nt(a, b):
    return lax.dot_general(a, b, (((1,), (1,)), ((), ())), preferred_element_type=F32)


def _ffn_body(x_ref, g_ref, wg_ref, wu_ref, wd_ref, o_ref, *, pre, post):
    x = x_ref[...]
    xn = _rms(x, g_ref[pre:pre + 1, :]).astype(BF16)
    acc = None
    for c in range(D_FF // FF_CHUNK):
        sl = slice(c * FF_CHUNK, (c + 1) * FF_CHUNK)
        g = _dot(xn, wg_ref[:, sl])
        u = _dot(xn, wu_ref[:, sl])
        h = (g * jax.nn.sigmoid(g) * u).astype(BF16)
        d = _dot(h, wd_ref[sl, :])
        acc = d if acc is None else acc + d
    o_ref[...] = x + 0.5 * _rms(acc, g_ref[post:post + 1, :])


def _ffn(x, norm_g, wg, wu, wd, l, j, tm):
    t = x.shape[0]
    once = pl.Buffered(1)
    return pl.pallas_call(
        functools.partial(_ffn_body, pre=3 * j + (0 if j == 0 else 1), post=3 * j + (1 if j == 0 else 2)),
        grid=(t // tm,),
        in_specs=[
            pl.BlockSpec((tm, D_MODEL), lambda i: (i, 0)),
            pl.BlockSpec((None, 6, D_MODEL), lambda i: (l, 0, 0)),
            pl.BlockSpec((None, None, D_MODEL, D_FF), lambda i: (l, j, 0, 0), pipeline_mode=once),
            pl.BlockSpec((None, None, D_MODEL, D_FF), lambda i: (l, j, 0, 0), pipeline_mode=once),
            pl.BlockSpec((None, None, D_FF, D_MODEL), lambda i: (l, j, 0, 0), pipeline_mode=once),
        ],
        out_specs=pl.BlockSpec((tm, D_MODEL), lambda i: (i, 0)),
        out_shape=jax.ShapeDtypeStruct((t, D_MODEL), F32),
        compiler_params=_cparams(("parallel",)),
        name="ffn",
    )(x, norm_g, wg, wu, wd)


def _rope(x, tab_ref, half):
    return (x * tab_ref[0]
            + pltpu.roll(x, LANES - half, 1) * tab_ref[1]
            + pltpu.roll(x, half, 1) * tab_ref[2])


def _split3(x):
    hi = x.astype(BF16)
    r1 = x - hi.astype(F32)
    mid = r1.astype(BF16)
    lo = (r1 - mid.astype(F32)).astype(BF16)
    return hi, mid, lo


def _inproj_body(h_ref, g_ref, wm_ref, wt_ref, bf_ref, qn_ref, kvn_ref, kvnc_ref, wuq_ref, wuk_ref,
                 ta_ref, tb_ref, sk_ref, sq_ref, ck1_ref, cq1_ref,
                 ak_ref, av_ref, ckv_ref, kr_ref, ck_ref, cv_ref, lf_ref,
                 qa_ref, akb_ref, avt_ref, qb_ref, kb_ref, ckvt_ref, qc_ref, kc_ref, cvt_ref,
                 carry, *, seq_tiles):
    tm = h_ref.shape[0]
    hn = _rms(h_ref[...], g_ref[2:3, :]).astype(BF16)
    z = _dot(hn, wm_ref[...])
    zt = _dot_nt(wt_ref[...], hn)
    lane = lax.broadcasted_iota(jnp.int32, (1, LANES), 1)
    low = lane < DC

    def slab(off, i=0):
        return z[:, off + i * LANES: off + (i + 1) * LANES]

    for h in range(H_A):
        q = _rope(slab(O_AQ, h), ta_ref, ROT_A // 2) * (DA ** -0.5 * LOG2E)
        qa_ref[:, (2 * h) * LANES:(2 * h + 1) * LANES] = jnp.where(low, q, 0.0).astype(BF16)
        qa_ref[:, (2 * h + 1) * LANES:(2 * h + 2) * LANES] = jnp.where(low, 0.0, q).astype(BF16)
    for i in range(2):
        k = _rope(slab(O_AK, i), ta_ref, ROT_A // 2)
        ak_ref[:, i * LANES:(i + 1) * LANES] = k
        akb_ref[:, i * LANES:(i + 1) * LANES] = k.astype(BF16)
    av_ref[...] = z[:, O_AV:O_AV + 2 * LANES]
    avt_ref[...] = zt[0:256].astype(BF16)

    qc = _rms(z[:, O_BCQ:O_BCQ + Q_RANK_B], qn_ref[...]).astype(BF16)
    qb = _dot(qc, wuq_ref[...])
    q_lat = _dot(qb[:, :H_B * DN_B].astype(BF16), wuk_ref[...])
    scale_b = (DN_B + DR_B) ** -0.5 * LOG2E
    for s in range(2):
        qr = _rope(qb[:, H_B * DN_B + s * LANES: H_B * DN_B + (s + 1) * LANES], tb_ref, DR_B // 2) * scale_b
        for hh in range(4):
            h = s * 4 + hh
            qb_ref[:, h * 256: h * 256 + LANES] = (q_lat[:, h * LANES:(h + 1) * LANES] * scale_b).astype(BF16)
            sel = (lane >= hh * DR_B) & (lane < (hh + 1) * DR_B)
            qb_ref[:, h * 256 + LANES: (h + 1) * 256] = jnp.where(sel, qr, 0.0).astype(BF16)
    ckv = _rms(slab(O_CKV), kvn_ref[...])
    ckv_ref[...] = ckv
    kr = _rope(slab(O_KR), tb_ref, DR_B // 2)
    kr_ref[...] = kr[:, :DR_B]
    kb_ref[:, :LANES] = ckv.astype(BF16)
    kb_ref[:, LANES:] = kr.astype(BF16)
    xt = zt[384:512]
    ckvt_ref[...] = (xt * lax.rsqrt(jnp.mean(xt * xt, axis=0, keepdims=True) + EPS) * kvnc_ref[...]).astype(BF16)

    ck = slab(O_CK)
    ck_ref[...] = ck
    cv_ref[...] = slab(O_CV)
    cvt_ref[...] = zt[256:384].astype(BF16)
    lf = _log_sigmoid(slab(O_CF) + bf_ref[...])
    lf_ref[...] = lf[:, :H_C]

    @pl.when(pl.program_id(0) % seq_tiles == 0)
    def _():
        carry[...] = jnp.zeros_like(carry)

    row = lax.broadcasted_iota(jnp.int32, (tm, LANES), 0)
    cum = lf
    s = 1
    while s < tm:
        cum = cum + jnp.where(row >= s, pltpu.roll(cum, s, 0), 0.0)
        s *= 2
    cum = cum + carry[...]
    carry[...] = cum[tm - 1:tm, :]
    parts = jnp.concatenate(_split3(cum * LOG2E), axis=1)
    kaug = _dot(parts, sk_ref[...]) + ck1_ref[...]
    qaug = _dot(parts, sq_ref[...]) + cq1_ref[...]
    for g in range(KV_C):
        kg = ck if g == 0 else pltpu.roll(ck, LANES - DC, 1)
        kc_ref[:, g * LANES:(g + 1) * LANES] = (jnp.where(low, kg, 0.0) + kaug[:, g * LANES:(g + 1) * LANES]).astype(BF16)
    for h in range(H_C):
        q = slab(O_CQ, h // 2)
        q = q if h % 2 == 0 else pltpu.roll(q, LANES - DC, 1)
        qc_ref[:, h * LANES:(h + 1) * LANES] = (jnp.where(low, q * (DC ** -0.5 * LOG2E), 0.0)
                                                + qaug[:, h * LANES:(h + 1) * LANES]).astype(BF16)


def _inproj(h, norm_g, p, l, tab_a, tab_b, tm, seq_tiles):
    t = h.shape[0]
    n_pos = tab_a.shape[1] // tm

    def full(a):
        nd = a.ndim - 1
        return pl.BlockSpec((None,) + a.shape[1:], lambda i: (l,) + (0,) * nd)

    def const(a):
        return pl.BlockSpec(a.shape, lambda i: (0,) * a.ndim)

    def rows(w):
        return pl.BlockSpec((tm, w), lambda i: (i, 0))

    def cols(w):
        return pl.BlockSpec((w, tm), lambda i: (0, i))

    tab_spec = pl.BlockSpec((3, tm, LANES), lambda i: (0, i % n_pos, 0))
    f32_w = (256, 256, 128, DR_B, 128, 128, H_C)
    bf_w = ((1024, 0), (256, 0), (256, 1), (H_B * 256, 0), (256, 0), (128, 1), (1024, 0), (256, 0), (128, 1))
    out_shape = ([jax.ShapeDtypeStruct((t, w), F32) for w in f32_w]
                 + [jax.ShapeDtypeStruct((w, t) if tr else (t, w), BF16) for w, tr in bf_w])
    out_specs = [rows(w) for w in f32_w] + [cols(w) if tr else rows(w) for w, tr in bf_w]
    names = ("w_main", "w_t", "b_f", "q_norm", "kv_norm", "kv_norm_col", "w_uq", "w_uk_bd")
    consts = (p["sel_k"], p["sel_q"], p["one_k"], p["one_q"])
    return pl.pallas_call(
        functools.partial(_inproj_body, seq_tiles=seq_tiles),
        grid=(t // tm,),
        in_specs=([rows(D_MODEL), full(norm_g)] + [full(p[n]) for n in names] + [tab_spec, tab_spec]
                  + [const(a) for a in consts]),
        out_specs=out_specs,
        out_shape=out_shape,
        scratch_shapes=[pltpu.VMEM((1, LANES), F32)],
        compiler_params=_cparams(("arbitrary",)),
        name="inproj",
    )(h, norm_g, *[p[n] for n in names], tab_a, tab_b, *consts)


def _flash_pass(streams, k_ref, vt_ref, qi):
    def step(r0, nk, carries, diagonal):
        out = []
        for (m, l, acc), (q_stack, kcols, vrows) in zip(carries, streams):
            st = _dot_nt(k_ref[pl.ds(r0, nk), kcols], q_stack)
            if diagonal:
                key = lax.broadcasted_iota(jnp.int32, st.shape, 0)
                qry = lax.broadcasted_iota(jnp.int32, st.shape, 1) & (TQ - 1)
                st = jnp.where(key <= qry, st, NEG)
            m_new = jnp.maximum(m, jnp.max(st, axis=0, keepdims=True))
            alpha = jnp.exp2(m - m_new)
            pt = jnp.exp2(st - m_new)
            l = alpha * l + jnp.sum(pt, axis=0, keepdims=True)
            acc = alpha * acc + _dot(vt_ref[vrows, pl.ds(r0, nk)], pt.astype(BF16))
            out.append((m_new, l, acc))
        return tuple(out)

    def init(q_stack, kcols, vrows):
        w = q_stack.shape[0]
        return (jnp.full((1, w), NEG, F32), jnp.zeros((1, w), F32), jnp.zeros((vrows.stop - vrows.start, w), F32))

    pairs = qi // 2
    carries = lax.fori_loop(0, pairs, lambda j, c: step(pl.multiple_of(j * 2 * TQ, 2 * TQ), 2 * TQ, c, False),
                            tuple(init(*s) for s in streams))
    carries = lax.fori_loop(0, qi % 2, lambda j, c: step(pl.multiple_of(pairs * 2 * TQ, 2 * TQ), TQ, c, False), carries)
    return [acc * (1.0 / l) for m, l, acc in step(pl.multiple_of(qi * TQ, TQ), TQ, carries, True)]


def _flash_heads(q_ref, width, n_heads, kcols, vrows, k_ref, vt_ref, qi, hpp=HEADS_PER_PASS):
    streams = []
    for first in range(0, n_heads, hpp):
        slabs = [q_ref[:, (first + v) * width:(first + v + 1) * width] for v in range(hpp)]
        streams.append((slabs[0] if len(slabs) == 1 else jnp.concatenate(slabs, axis=0), kcols(first), vrows(first)))
    outs = _flash_pass(streams, k_ref, vt_ref, qi)
    return [ot[:, v * TQ:(v + 1) * TQ] for ot in outs for v in range(hpp)]


def _to_rows(yt):
    eye = (lax.broadcasted_iota(jnp.int32, (TQ, TQ), 0) == lax.broadcasted_iota(jnp.int32, (TQ, TQ), 1))
    return _dot_nt(jnp.where(eye, 1.0, 0.0).astype(BF16), yt).astype(BF16)


def _flash_a_body(q_ref, k_ref, vt_ref, dl_ref, sub_ref, o_ref, *, lam_init):
    qi = pl.program_id(1)
    dl = dl_ref[...]
    lam = (jnp.exp(jnp.sum(dl[0:1] * dl[1:2], axis=-1, keepdims=True))
           - jnp.exp(jnp.sum(dl[2:3] * dl[3:4], axis=-1, keepdims=True)) + lam_init)
    def group(slab):
        g = slab // (2 * H_A // KV_A)
        return slice(g * 2 * DA, (g + 1) * 2 * DA)

    outs = _flash_heads(q_ref, LANES, 2 * H_A, group, group, k_ref, vt_ref, qi)
    for h in range(H_A):
        y = outs[2 * h] - lam * outs[2 * h + 1]
        y = y * lax.rsqrt(jnp.mean(y * y, axis=0, keepdims=True) + EPS) * sub_ref[...] * (1.0 - lam_init)
        o_ref[:, h * 2 * DA:(h + 1) * 2 * DA] = _to_rows(y.astype(BF16))


def _flash_b_body(q_ref, k_ref, vt_ref, wuvt_ref, o_ref):
    qi = pl.program_id(1)
    outs = _flash_heads(q_ref, 256, H_B, lambda h: slice(0, 256), lambda h: slice(0, KV_RANK_B), k_ref, vt_ref, qi,
                        hpp=H_B)
    ybt = _dot(wuvt_ref[...], jnp.concatenate([o.astype(BF16) for o in outs], axis=0))
    o_ref[...] = _to_rows(ybt.astype(BF16))


def _flash_c_body(q_ref, k_ref, vt_ref, o_ref):
    qi = pl.program_id(1)
    per = H_C // KV_C
    outs = _flash_heads(q_ref, LANES, H_C, lambda h: slice((h // per) * LANES, (h // per + 1) * LANES),
                        lambda h: slice((h // per) * DC, (h // per + 1) * DC), k_ref, vt_ref, qi)
    o_ref[...] = _to_rows(jnp.concatenate([o.astype(BF16) for o in outs], axis=0))


def _flash_call(body, name, batch, seq, q, k, vt, extra, extra_specs):
    t = q.shape[0]
    nq = seq // TQ
    return pl.pallas_call(
        body,
        grid=(batch, nq),
        in_specs=[pl.BlockSpec((TQ, q.shape[1]), lambda b, i: (b * nq + i, 0)),
                  pl.BlockSpec((seq, k.shape[1]), lambda b, i: (b, 0)),
                  pl.BlockSpec((vt.shape[0], seq), lambda b, i: (0, b))] + extra_specs,
        out_specs=pl.BlockSpec((TQ, BRANCH_W), lambda b, i: (b * nq + i, 0)),
        out_shape=jax.ShapeDtypeStruct((t, BRANCH_W), BF16),
        compiler_params=_cparams(("parallel", "arbitrary")),
        name=name,
    )(q, k, vt, *extra)


def _decode_body(pt_ref, *refs, l, lam_init, n_steps, n_pages, n_total):
    n = PAGES_PER_STEP
    hbm = refs[:N_CACHES]
    (qa_ref, ka_ref, va_ref, ql_ref, qr_ref, kl_ref, krn_ref, qc_ref, kc_ref, vc_ref, lfn_ref, qrb_ref, qcb_ref,
     dl_ref, sub_ref, wuv_ref) = refs[N_CACHES:N_CACHES + 16]
    oa_ref, ob_ref, oc_ref = refs[N_CACHES + 16:N_CACHES + 19]
    bufs = refs[N_CACHES + 19:2 * N_CACHES + 19]
    sems = refs[2 * N_CACHES + 19]
    ma, la, acca, mb, lb, accb, mc, lc, accc, carry = refs[2 * N_CACHES + 20:]
    ak_b, av_b, lat_b, kr_b, ck_b, cv_b, lf_b = bufs
    b = pl.program_id(0)
    step = pl.program_id(1)
    t = b * n_steps + step
    slot = lax.rem(t, 2)

    def page_copy(c, page, sl, i):
        return pltpu.make_async_copy(hbm[c].at[l, page], bufs[c].at[sl, i], sems.at[sl, c])

    def start_step(bb, ss, sl):
        for i in range(n):
            page = pt_ref[bb, n_pages - 1 - (ss * n + i)]
            for c in range(N_CACHES):
                page_copy(c, page, sl, i).start()

    @pl.when(t == 0)
    def _():
        start_step(0, 0, 0)

    @pl.when(t + 1 < n_total)
    def _():
        last = step == n_steps - 1
        start_step(jnp.where(last, b + 1, b), jnp.where(last, 0, step + 1), 1 - slot)

    for i in range(n):
        for c in range(N_CACHES):
            page_copy(c, 0, slot, i).wait()

    qa = qa_ref[...]
    ql = ql_ref[...]
    qr = qr_ref[...]
    qc = qc_ref[...]

    @pl.when(step == 0)
    def _():
        ma[...] = jnp.sum(qa * ka_ref[...], axis=-1, keepdims=True)
        la[...] = jnp.ones_like(la)
        acca[...] = va_ref[...]
        mb[...] = (jnp.sum(ql * kl_ref[...], axis=-1, keepdims=True)
                   + jnp.sum(qr * krn_ref[...], axis=-1, keepdims=True))
        lb[...] = jnp.ones_like(lb)
        accb[...] = jnp.broadcast_to(kl_ref[...], accb.shape)
        mc[...] = jnp.sum(qc * kc_ref[...], axis=-1, keepdims=True)
        lc[...] = jnp.ones_like(lc)
        accc[...] = vc_ref[...]
        carry[...] = lfn_ref[...]

    def update(s, m_ref, l_ref, acc_ref, pv):
        m_old = m_ref[...]
        m_new = jnp.maximum(m_old, jnp.max(s, axis=-1, keepdims=True))
        alpha = jnp.exp2(m_old - m_new)
        pr = jnp.exp2(s - m_new)
        l_ref[...] = alpha * l_ref[...] + jnp.sum(pr, axis=-1, keepdims=True)
        acc_ref[...] = alpha * acc_ref[...] + pv(pr)
        m_ref[...] = m_new

    def pages_to_lanes(x):
        return jnp.concatenate([x[i * 8:(i + 1) * 8] for i in range(n)], axis=1)

    def pages_to_rows(x):
        return jnp.concatenate([x[:, i * PAGE:(i + 1) * PAGE] for i in range(n)], axis=0)

    ak_all = ak_b[slot].reshape(n * KV_A * PAGE, 2 * DA)
    av_all = av_b[slot].reshape(n * KV_A * PAGE, 2 * DA)
    row_a = lax.broadcasted_iota(jnp.int32, (8, n * KV_A * PAGE), 0)
    col_a = lax.broadcasted_iota(jnp.int32, (8, n * KV_A * PAGE), 1)
    s_a = jnp.where((col_a & 1) == (row_a >> 2), _dot_nt(qa, ak_all), NEG)
    update(s_a, ma, la, acca, lambda pr: _dot(pr, av_all))

    lat_all = lat_b[slot].reshape(n * PAGE, KV_RANK_B)
    s_b = _dot_nt(ql, lat_all) + pages_to_lanes(_dot(qrb_ref[...], kr_b[slot].reshape(n * DR_B, PAGE)))
    update(s_b, mb, lb, accb, lambda pr: _dot(pr, lat_all))

    lane = lax.broadcasted_iota(jnp.int32, (H_C, PAGE), 1)
    bias = []
    run = carry[...]
    for i in range(n):
        x = lf_b[slot, i]
        suf = x
        sh = 1
        while sh < PAGE:
            suf = suf + jnp.where(lane + sh < PAGE, pltpu.roll(suf, PAGE - sh, 1), 0.0)
            sh *= 2
        bias.append(((suf - x) + run) * LOG2E)
        run = run + suf[:, 0:1]
    carry[...] = run
    s_c = (pages_to_lanes(_dot(qcb_ref[...], ck_b[slot].reshape(n * KV_C * DC, PAGE)))
           + jnp.concatenate(bias, axis=1))
    cv_all = cv_b[slot].reshape(n * KV_C * DC, PAGE)

    def pv_c(pr):
        big = _dot_nt(pages_to_rows(pr), cv_all)
        w = KV_C * DC
        return sum(big[i * 8:(i + 1) * 8, i * w:(i + 1) * w] for i in range(n))

    update(s_c, mc, lc, accc, pv_c)

    @pl.when(step == n_steps - 1)
    def _():
        dl = dl_ref[...]
        lam = (jnp.exp(jnp.sum(dl[0:1] * dl[1:2], axis=-1, keepdims=True))
               - jnp.exp(jnp.sum(dl[2:3] * dl[3:4], axis=-1, keepdims=True)) + lam_init)
        oa = acca[...] / la[...]
        for h in range(H_A):
            y = oa[2 * h:2 * h + 1] - lam * oa[2 * h + 1:2 * h + 2]
            oa_ref[h:h + 1, :] = _rms(y, sub_ref[...]) * (1.0 - lam_init)
        ob = (accb[...] / lb[...]).astype(BF16)
        row = lax.broadcasted_iota(jnp.int32, (H_B, DV_B), 0)
        yb = jnp.zeros((H_B, DV_B), F32)
        for h in range(H_B):
            yb = jnp.where(row == h, _dot(ob, wuv_ref[h]), yb)
        ob_ref[...] = yb
        oc = accc[...] / lc[...]
        oc_ref[...] = jnp.where(row < H_C // KV_C, oc[:, :DC], oc[:, DC:])


def _decode(l, page_table, caches, samp, p, lam_init):
    nb, n_pages = page_table.shape
    n = PAGES_PER_STEP
    n_steps = n_pages // n

    def samp_spec(a):
        return pl.BlockSpec((None,) + a.shape[1:], lambda b, s, pt: (b,) + (0,) * (a.ndim - 1))

    def layer_spec(a):
        return pl.BlockSpec((None,) + a.shape[1:], lambda b, s, pt: (l,) + (0,) * (a.ndim - 1))

    layer_args = (p["diff_lambda"], p["diff_subln"], p["w_uv_h"])
    in_specs = ([pl.BlockSpec(memory_space=pl.ANY)] * N_CACHES + [samp_spec(a) for a in samp]
                + [layer_spec(a) for a in layer_args])
    out_shape = [jax.ShapeDtypeStruct((nb, H_A, 2 * DA), F32), jax.ShapeDtypeStruct((nb, H_B, DV_B), F32),
                 jax.ShapeDtypeStruct((nb, H_C, DC), F32)]
    out_specs = [pl.BlockSpec((None,) + s.shape[1:], lambda b, s_, pt: (b, 0, 0)) for s in out_shape]
    col = pltpu.VMEM((8, 1), F32)
    scratch = ([pltpu.VMEM((2, n) + a.shape[2:], F32) for a in caches] + [pltpu.SemaphoreType.DMA((2, N_CACHES))]
               + [col, col, pltpu.VMEM((8, 2 * DA), F32), col, col, pltpu.VMEM((8, KV_RANK_B), F32),
                  col, col, pltpu.VMEM((8, KV_C * DC), F32), col])
    grid_spec = pltpu.PrefetchScalarGridSpec(
        num_scalar_prefetch=1, grid=(nb, n_steps), in_specs=in_specs, out_specs=out_specs, scratch_shapes=scratch)
    return pl.pallas_call(
        functools.partial(_decode_body, l=l, lam_init=lam_init, n_steps=n_steps, n_pages=n_pages,
                          n_total=nb * n_steps),
        grid_spec=grid_spec,
        out_shape=out_shape,
        compiler_params=_cparams(("arbitrary", "arbitrary")),
        name="decode_attn",
    )(page_table, *caches, *samp, *layer_args)


def _merge_body(h_ref, g_ref, ya_ref, yb_ref, yc_ref, wgate_ref, wbr_ref, wout_ref, o_ref):
    h = h_ref[...]
    hn = _rms(h, g_ref[2:3, :]).astype(BF16)
    merged = None
    for n, y_ref in enumerate((ya_ref, yb_ref, yc_ref)):
        gate = jax.nn.sigmoid(_dot(hn, wgate_ref[:, n * D_MODEL:(n + 1) * D_MODEL]))
        term = gate * _dot(y_ref[...], wbr_ref[n])
        merged = term if merged is None else merged + term
    m = _dot(merged.astype(BF16), wout_ref[...])
    o_ref[...] = h + _rms(m, g_ref[3:4, :])


def _merge(h, norm_g, ya, yb, yc, p, l, tm):
    t = h.shape[0]

    def full(a):
        nd = a.ndim - 1
        return pl.BlockSpec((None,) + a.shape[1:], lambda i: (l,) + (0,) * nd)

    def rows(w):
        return pl.BlockSpec((tm, w), lambda i: (i, 0))

    return pl.pallas_call(
        _merge_body,
        grid=(t // tm,),
        in_specs=[rows(D_MODEL), full(norm_g), rows(BRANCH_W), rows(BRANCH_W), rows(BRANCH_W),
                  full(p["w_gate"]), full(p["w_branch"]), full(p["w_out"])],
        out_specs=rows(D_MODEL),
        out_shape=jax.ShapeDtypeStruct((t, D_MODEL), F32),
        compiler_params=_cparams(("parallel",)),
        name="merge",
    )(h, norm_g, ya, yb, yc, p["w_gate"], p["w_branch"], p["w_out"])


def _rope_tables(pos, rot, period, theta):
    half = rot // 2
    inv = 1.0 / (theta ** (jnp.arange(0, rot, 2, dtype=F32) / rot))
    ang = pos.astype(F32)[:, None] * inv[None, :]
    cos, sin = jnp.cos(ang), jnp.sin(ang)
    n = pos.shape[0]
    pad = jnp.zeros((n, period - rot), F32)
    c = jnp.concatenate([cos, cos, jnp.ones((n, period - rot), F32)], axis=1)
    s1 = jnp.concatenate([-sin, jnp.zeros((n, half), F32), pad], axis=1)
    s2 = jnp.concatenate([jnp.zeros((n, half), F32), sin, pad], axis=1)
    rep = LANES // period
    return jnp.stack([jnp.tile(c, (1, rep)), jnp.tile(s1, (1, rep)), jnp.tile(s2, (1, rep))])


def _bias_selectors():
    per = H_C // KV_C
    sel_k = np.zeros((3 * LANES, KV_C * LANES), np.float32)
    sel_q = np.zeros((3 * LANES, H_C * LANES), np.float32)
    one_k = np.zeros((1, KV_C * LANES), np.float32)
    one_q = np.zeros((1, H_C * LANES), np.float32)
    for h in range(H_C):
        g, hh = divmod(h, per)
        for part in range(3):
            sel_k[part * LANES + h, g * LANES + AUG_K0 + 3 * hh + part] = -1.0
            sel_q[part * LANES + h, h * LANES + AUG_Q0 + part] = 1.0
            one_q[0, h * LANES + AUG_K0 + 3 * hh + part] = 1.0
    for g in range(KV_C):
        for part in range(3):
            one_k[0, g * LANES + AUG_Q0 + part] = 1.0
    return jnp.asarray(sel_k, BF16), jnp.asarray(sel_q, BF16), jnp.asarray(one_k), jnp.asarray(one_q)


def _prep_params(w_in, b_forget, diff_lambda, diff_subln, mla_q_norm, mla_kv_norm, mla_w_uq, mla_w_uk, mla_w_uv,
                 w_branch, w_out):
    depth = w_in.shape[0]
    widths = (H_A * 2 * DA, KV_A * 2 * DA, KV_A * 2 * DA, Q_RANK_B, KV_RANK_B, DR_B, H_C * DC, KV_C * DC, KV_C * DC,
              H_C, N_BRANCH * D_MODEL)
    offs = np.cumsum((0,) + widths)
    seg = {n: w_in[:, :, offs[i]:offs[i + 1]] for i, n in enumerate(
        ("a_q", "a_k", "a_v", "b_cq", "b_ckv", "b_kr", "c_q", "c_k", "c_v", "c_f", "gate"))}
    zeros = jnp.zeros((depth, D_MODEL, LANES - H_C), w_in.dtype)
    w_main = jnp.concatenate(
        [seg["a_q"], seg["a_k"], seg["a_v"], seg["b_cq"], seg["b_ckv"], seg["c_k"], seg["c_q"], seg["c_v"],
         seg["b_kr"], seg["b_kr"], seg["b_kr"], seg["b_kr"], seg["c_f"], zeros], axis=2).astype(BF16)
    assert w_main.shape[2] == N_MAIN
    w_t = jnp.transpose(jnp.concatenate([seg["a_v"], seg["c_v"], seg["b_ckv"]], axis=2), (0, 2, 1)).astype(BF16)
    uq = mla_w_uq.reshape(depth, Q_RANK_B, H_B, DN_B + DR_B)
    w_uq = jnp.concatenate([uq[..., :DN_B].reshape(depth, Q_RANK_B, H_B * DN_B),
                            uq[..., DN_B:].reshape(depth, Q_RANK_B, H_B * DR_B)], axis=2).astype(BF16)
    eye = jnp.eye(H_B, dtype=w_in.dtype)
    w_uk_bd = jnp.einsum("lchn,hg->lhngc", mla_w_uk, eye).reshape(depth, H_B * DN_B, H_B * KV_RANK_B).astype(BF16)
    w_uv_bdt = jnp.einsum("lchv,hg->lgvhc", mla_w_uv, eye).reshape(depth, H_B * DV_B, H_B * KV_RANK_B).astype(BF16)
    w_uv_h = jnp.transpose(mla_w_uv, (0, 2, 1, 3)).astype(BF16)
    sel_k, sel_q, one_k, one_q = _bias_selectors()
    return {
        "w_main": w_main,
        "w_t": w_t,
        "b_f": jnp.pad(b_forget, ((0, 0), (0, LANES - H_C)))[:, None, :],
        "q_norm": mla_q_norm[:, None, :],
        "kv_norm": mla_kv_norm[:, None, :],
        "kv_norm_col": mla_kv_norm[:, :, None],
        "w_uq": w_uq,
        "w_uk_bd": w_uk_bd,
        "w_uv_bdt": w_uv_bdt,
        "w_uv_h": w_uv_h,
        "w_gate": seg["gate"].astype(BF16),
        "w_branch": w_branch.astype(BF16),
        "w_out": w_out.astype(BF16),
        "diff_lambda": diff_lambda,
        "diff_subln": diff_subln[:, None, :],
        "diff_subln_col": diff_subln[:, :, None],
        "sel_k": sel_k, "sel_q": sel_q, "one_k": one_k, "one_q": one_q,
    }


def kernel(x_prompt, x_sample, cache_a_k, cache_a_v, cache_b_lat, cache_b_kr, cache_c_k, cache_c_v, cache_c_logf, page_table, norm_g, ffn_w_gate, ffn_w_up, ffn_w_down, w_in, b_forget, diff_lambda, diff_subln, mla_q_norm, mla_kv_norm, mla_w_uq, mla_w_uk, mla_w_uv, w_branch, w_out):
    batch, seq, _ = x_prompt.shape
    nb = x_sample.shape[0]
    depth = norm_g.shape[0]
    past_len = page_table.shape[1] * PAGE
    tp = batch * seq
    tm_p = 512

    p = _prep_params(w_in, b_forget, diff_lambda, diff_subln, mla_q_norm, mla_kv_norm, mla_w_uq, mla_w_uk, mla_w_uv,
                     w_branch, w_out)
    wg, wu, wd = ffn_w_gate.astype(BF16), ffn_w_up.astype(BF16), ffn_w_down.astype(BF16)
    pos_p = jnp.arange(seq, dtype=jnp.int32)
    pos_s = jnp.full((nb,), past_len, dtype=jnp.int32)
    tabs_p = (_rope_tables(pos_p, ROT_A, DA, THETA_A), _rope_tables(pos_p, DR_B, DR_B, THETA_B))
    tabs_s = (_rope_tables(pos_s, ROT_A, DA, THETA_A), _rope_tables(pos_s, DR_B, DR_B, THETA_B))
    n_pool = cache_a_k.shape[1]
    caches = (cache_a_k.reshape(depth, n_pool, KV_A * PAGE, 2 * DA),
              cache_a_v.reshape(depth, n_pool, KV_A * PAGE, 2 * DA),
              cache_b_lat,
              jnp.transpose(cache_b_kr, (0, 1, 3, 2)),
              jnp.transpose(cache_c_k, (0, 1, 3, 4, 2)).reshape(depth, n_pool, KV_C * DC, PAGE),
              jnp.transpose(cache_c_v, (0, 1, 3, 4, 2)).reshape(depth, n_pool, KV_C * DC, PAGE),
              jnp.transpose(cache_c_logf, (0, 1, 3, 2)))

    xp = x_prompt.reshape(tp, D_MODEL)
    xs = x_sample.reshape(nb, D_MODEL)
    rows_p = [[] for _ in range(7)]
    rows_s = [[] for _ in range(7)]
    for l in range(depth):
        lam_init = 0.8 - 0.6 * math.exp(-0.3 * l)
        lay = lambda a: pl.BlockSpec((None,) + a.shape[1:], lambda b, i: (l,) + (0,) * (a.ndim - 1))
        hs = _ffn(xs, norm_g, wg, wu, wd, l, 0, nb)
        (ak, av, ckv, kr, ck, cv, lf, qa, akb, avt, qb, kb, ckvt, qc, kc, cvt) = _inproj(
            hs, norm_g, p, l, tabs_s[0], tabs_s[1], nb, 1)
        rep_a = 2 * H_A // KV_A
        qa3 = qa.astype(F32).reshape(nb, 2 * H_A, 2 * DA)
        ka = jnp.repeat(akb.astype(F32).reshape(nb, KV_A, 2 * DA), rep_a, axis=1)
        va = jnp.repeat(av.reshape(nb, KV_A, 2 * DA), rep_a, axis=1)
        qb3 = qb.astype(F32).reshape(nb, H_B, 256)
        ql = qb3[:, :, :KV_RANK_B]
        qr = qb3[:, :, KV_RANK_B:].reshape(nb, H_B, LANES // DR_B, DR_B).sum(axis=2)
        kbf = kb.astype(F32)
        kl = kbf[:, None, :KV_RANK_B]
        krn = kbf[:, None, KV_RANK_B:KV_RANK_B + DR_B]
        qc1 = qc.astype(F32).reshape(nb, H_C, LANES)[:, :, :DC]
        in_g0 = (jnp.arange(H_C) < H_C // KV_C)[None, :, None]
        qc3 = jnp.concatenate([jnp.where(in_g0, qc1, 0.0), jnp.where(in_g0, 0.0, qc1)], axis=2)
        kc3 = jnp.tile(jnp.repeat(ck.astype(BF16).astype(F32).reshape(nb, KV_C, DC), H_C // KV_C, axis=1), (1, 1, KV_C))
        vc3 = jnp.tile(jnp.repeat(cv.reshape(nb, KV_C, DC), H_C // KV_C, axis=1), (1, 1, KV_C))
        lfn = lf.reshape(nb, H_C, 1)
        eye_p = jnp.eye(PAGES_PER_STEP, dtype=F32)
        qrb = jnp.einsum("bhr,pq->bphqr", qr, eye_p).reshape(nb, PAGES_PER_STEP * H_B, PAGES_PER_STEP * DR_B)
        qcb = jnp.einsum("bhc,pq->bphqc", qc3, eye_p).reshape(nb, PAGES_PER_STEP * H_C, PAGES_PER_STEP * KV_C * DC)
        oa, ob, oc = _decode(l, page_table, caches, (qa3, ka, va, ql, qr, kl, krn, qc3, kc3, vc3, lfn, qrb, qcb), p,
                             lam_init)
        h2 = _merge(hs, norm_g, oa.reshape(nb, BRANCH_W).astype(BF16), ob.reshape(nb, BRANCH_W).astype(BF16),
                    oc.reshape(nb, BRANCH_W).astype(BF16), p, l, nb)
        xs = _ffn(h2, norm_g, wg, wu, wd, l, 1, nb)
        for dst, a in zip(rows_s, (ak.reshape(nb, 1, KV_A, 2 * DA), av.reshape(nb, 1, KV_A, 2 * DA),
                                   ckv.reshape(nb, 1, KV_RANK_B), kr.reshape(nb, 1, DR_B),
                                   ck.reshape(nb, 1, KV_C, DC), cv.reshape(nb, 1, KV_C, DC),
                                   lf.reshape(nb, 1, H_C))):
            dst.append(a)

        hp = _ffn(xp, norm_g, wg, wu, wd, l, 0, tm_p)
        (ak, av, ckv, kr, ck, cv, lf, qa, akb, avt, qb, kb, ckvt, qc, kc, cvt) = _inproj(
            hp, norm_g, p, l, tabs_p[0], tabs_p[1], tm_p, seq // tm_p)
        ya = _flash_call(functools.partial(_flash_a_body, lam_init=lam_init), "flash_a", batch, seq, qa, akb, avt,
                         (p["diff_lambda"], p["diff_subln_col"]), [lay(p["diff_lambda"]), lay(p["diff_subln_col"])])
        yb = _flash_call(_flash_b_body, "flash_b", batch, seq, qb, kb, ckvt, (p["w_uv_bdt"],), [lay(p["w_uv_bdt"])])
        yc = _flash_call(_flash_c_body, "flash_c", batch, seq, qc, kc, cvt, (), [])
        h2 = _merge(hp, norm_g, ya, yb, yc, p, l, tm_p)
        xp = _ffn(h2, norm_g, wg, wu, wd, l, 1, tm_p)
        for dst, a in zip(rows_p, (ak.reshape(batch, seq, KV_A, 2 * DA), av.reshape(batch, seq, KV_A, 2 * DA),
                                   ckv.reshape(batch, seq, KV_RANK_B), kr.reshape(batch, seq, DR_B),
                                   ck.reshape(batch, seq, KV_C, DC), cv.reshape(batch, seq, KV_C, DC),
                                   lf.reshape(batch, seq, H_C))):
            dst.append(a)

    out = [xp.reshape(batch, seq, D_MODEL), xs.reshape(nb, 1, D_MODEL)]
    for rp, rs in zip(rows_p, rows_s):
        out.append(jnp.stack(rp))
        out.append(jnp.stack(rs))
    return tuple(out)
```

```python
import functools
import math

import numpy as np
import jax
import jax.numpy as jnp
from jax import lax
from jax.experimental import pallas as pl
from jax.experimental.pallas import tpu as pltpu

F32 = jnp.float32
BF16 = jnp.bfloat16

D_MODEL = 1024
D_FF = 2816
EPS = 1e-6
PAGE = 128
H_A, KV_A, DA, ROT_A, THETA_A = 4, 2, 64, 16, 500000.0
H_B, Q_RANK_B, KV_RANK_B, DN_B, DR_B, DV_B, THETA_B = 8, 256, 128, 64, 32, 64, 10000.0
H_C, KV_C, DC = 8, 2, 64
BRANCH_W = 512
N_BRANCH = 3
LANES = 128
NEG = -1e30
LOG2E = math.log2(math.e)

O_AQ, O_AK, O_AV, O_BCQ, O_CKV, O_CK, O_CQ, O_CV, O_KR, O_CF, N_MAIN = (
    0, 512, 768, 1024, 1280, 1408, 1536, 2048, 2176, 2304, 2432)
AUG_K0 = DC
AUG_Q0 = DC + 3 * (H_C // KV_C)
ZT_CV, ZT_CKV, ZT_CK, ZT_KR, ZT_CF, ZT_ROWS = 256, 384, 512, 640, 672, 680

FF_CHUNK = 1408
TQ = 256
HEADS_PER_PASS = 4
VMEM_LIMIT = 56 * 1024 * 1024
PAGES_PER_STEP = 16
N_CACHES = 7
ONES_ROWS = 16
VA_ROWS, VB_ROWS, VC_ROWS = 2 * DA + ONES_ROWS, KV_RANK_B + ONES_ROWS, DC + ONES_ROWS


def _cparams(sem):
    return pltpu.CompilerParams(dimension_semantics=sem, vmem_limit_bytes=VMEM_LIMIT)


def _rms(x, g):
    return x * lax.rsqrt(jnp.mean(x * x, axis=-1, keepdims=True) + EPS) * g


def _log_sigmoid(x):
    return jnp.minimum(x, 0.0) - jnp.log(1.0 + jnp.exp(-jnp.abs(x)))


def _dot(a, b):
    return jnp.dot(a, b, preferred_element_type=F32)


def _dot_nt(a, b):
    return lax.dot_general(a, b, (((1,), (1,)), ((), ())), preferred_element_type=F32)


def _ffn_body(x_ref, g_ref, wg_ref, wu_ref, wd_ref, o_ref, *, pre, post):
    x = x_ref[...]
    xn = _rms(x, g_ref[pre:pre + 1, :]).astype(BF16)
    acc = None
    for c in range(D_FF // FF_CHUNK):
        sl = slice(c * FF_CHUNK, (c + 1) * FF_CHUNK)
        g = _dot(xn, wg_ref[:, sl])
        u = _dot(xn, wu_ref[:, sl])
        h = (g * jax.nn.sigmoid(g) * u).astype(BF16)
        d = _dot(h, wd_ref[sl, :])
        acc = d if acc is None else acc + d
    o_ref[...] = x + 0.5 * _rms(acc, g_ref[post:post + 1, :])


def _ffn(x, norm_g, wg, wu, wd, l, j, tm):
    t = x.shape[0]
    once = pl.Buffered(1)
    return pl.pallas_call(
        functools.partial(_ffn_body, pre=3 * j + (0 if j == 0 else 1), post=3 * j + (1 if j == 0 else 2)),
        grid=(t // tm,),
        in_specs=[
            pl.BlockSpec((tm, D_MODEL), lambda i: (i, 0)),
            pl.BlockSpec((None, 6, D_MODEL), lambda i: (l, 0, 0)),
            pl.BlockSpec((None, None, D_MODEL, D_FF), lambda i: (l, j, 0, 0), pipeline_mode=once),
            pl.BlockSpec((None, None, D_MODEL, D_FF), lambda i: (l, j, 0, 0), pipeline_mode=once),
            pl.BlockSpec((None, None, D_FF, D_MODEL), lambda i: (l, j, 0, 0), pipeline_mode=once),
        ],
        out_specs=pl.BlockSpec((tm, D_MODEL), lambda i: (i, 0)),
        out_shape=jax.ShapeDtypeStruct((t, D_MODEL), F32),
        compiler_params=_cparams(("parallel",)),
        name="ffn",
    )(x, norm_g, wg, wu, wd)


def _rope(x, tab_ref, half):
    return (x * tab_ref[0]
            + pltpu.roll(x, LANES - half, 1) * tab_ref[1]
            + pltpu.roll(x, half, 1) * tab_ref[2])


def _split3(x):
    hi = x.astype(BF16)
    r1 = x - hi.astype(F32)
    mid = r1.astype(BF16)
    lo = (r1 - mid.astype(F32)).astype(BF16)
    return hi, mid, lo


def _inproj_body(h_ref, g_ref, wm_ref, wt_ref, bf_ref, qn_ref, kvn_ref, kvnc_ref, wuq_ref, wuk_ref,
                 ta_ref, tb_ref, tbt_ref, bfc_ref, sk_ref, sq_ref, ck1_ref, cq1_ref,
                 ak_ref, av_ref, ckv_ref, krt_ref, ckt_ref, cvtf_ref, lft_ref,
                 qa_ref, akb_ref, avt_ref, qb_ref, kb_ref, ckvt_ref, qc_ref, kc_ref, cvt_ref,
                 carry, *, seq_tiles):
    tm = h_ref.shape[0]
    hn = _rms(h_ref[...], g_ref[2:3, :]).astype(BF16)
    z = _dot(hn, wm_ref[...])
    zt = _dot_nt(wt_ref[...], hn)
    lane = lax.broadcasted_iota(jnp.int32, (1, LANES), 1)
    low = lane < DC

    def slab(off, i=0):
        return z[:, off + i * LANES: off + (i + 1) * LANES]

    for h in range(H_A):
        q = _rope(slab(O_AQ, h), ta_ref, ROT_A // 2) * (DA ** -0.5 * LOG2E)
        qa_ref[:, (2 * h) * LANES:(2 * h + 1) * LANES] = jnp.where(low, q, 0.0).astype(BF16)
        qa_ref[:, (2 * h + 1) * LANES:(2 * h + 2) * LANES] = jnp.where(low, 0.0, q).astype(BF16)
    for i in range(2):
        k = _rope(slab(O_AK, i), ta_ref, ROT_A // 2)
        ak_ref[pl.ds(i, tm, stride=KV_A), :] = k
        akb_ref[:, i * LANES:(i + 1) * LANES] = k.astype(BF16)
        av_ref[pl.ds(i, tm, stride=KV_A), :] = slab(O_AV, i)
    ones = jnp.ones((ONES_ROWS, tm), BF16)
    for g in range(KV_A):
        avt_ref[g * VA_ROWS:g * VA_ROWS + 2 * DA, :] = zt[g * 2 * DA:(g + 1) * 2 * DA].astype(BF16)
        avt_ref[g * VA_ROWS + 2 * DA:(g + 1) * VA_ROWS, :] = ones

    qc = _rms(z[:, O_BCQ:O_BCQ + Q_RANK_B], qn_ref[...]).astype(BF16)
    qb = _dot(qc, wuq_ref[...])
    q_lat = _dot(qb[:, :H_B * DN_B].astype(BF16), wuk_ref[...])
    scale_b = (DN_B + DR_B) ** -0.5 * LOG2E
    for s in range(2):
        qr = _rope(qb[:, H_B * DN_B + s * LANES: H_B * DN_B + (s + 1) * LANES], tb_ref, DR_B // 2) * scale_b
        for hh in range(4):
            h = s * 4 + hh
            qb_ref[:, h * 256: h * 256 + LANES] = (q_lat[:, h * LANES:(h + 1) * LANES] * scale_b).astype(BF16)
            sel = (lane >= hh * DR_B) & (lane < (hh + 1) * DR_B)
            qb_ref[:, h * 256 + LANES: (h + 1) * 256] = jnp.where(sel, qr, 0.0).astype(BF16)
    ckv = _rms(slab(O_CKV), kvn_ref[...])
    ckv_ref[...] = ckv
    kr = _rope(slab(O_KR), tb_ref, DR_B // 2)
    kb_ref[:, :LANES] = ckv.astype(BF16)
    kb_ref[:, LANES:] = kr.astype(BF16)
    half = DR_B // 2
    x1, x2 = zt[ZT_KR:ZT_KR + half], zt[ZT_KR + half:ZT_KR + DR_B]
    krt_ref[0:half, :] = x1 * tbt_ref[0] - x2 * tbt_ref[1]
    krt_ref[half:, :] = x2 * tbt_ref[0] + x1 * tbt_ref[1]
    xt = zt[ZT_CKV:ZT_CKV + KV_RANK_B]
    ckvt_ref[0:KV_RANK_B, :] = (xt * lax.rsqrt(jnp.mean(xt * xt, axis=0, keepdims=True) + EPS)
                                * kvnc_ref[...]).astype(BF16)
    ckvt_ref[KV_RANK_B:, :] = ones

    ck = slab(O_CK)
    ckt_ref[...] = zt[ZT_CK:ZT_CK + KV_C * DC]
    cvtf_ref[...] = zt[ZT_CV:ZT_CV + KV_C * DC]
    for g in range(KV_C):
        cvt_ref[g * VC_ROWS:g * VC_ROWS + DC, :] = zt[ZT_CV + g * DC:ZT_CV + (g + 1) * DC].astype(BF16)
        cvt_ref[g * VC_ROWS + DC:(g + 1) * VC_ROWS, :] = ones
    lf = _log_sigmoid(slab(O_CF) + bf_ref[...])
    lft_ref[...] = _log_sigmoid(zt[ZT_CF:ZT_CF + H_C] + bfc_ref[...])

    @pl.when(pl.program_id(0) % seq_tiles == 0)
    def _():
        carry[...] = jnp.zeros_like(carry)

    row = lax.broadcasted_iota(jnp.int32, (tm, LANES), 0)
    cum = lf
    s = 1
    while s < tm:
        cum = cum + jnp.where(row >= s, pltpu.roll(cum, s, 0), 0.0)
        s *= 2
    cum = cum + carry[...]
    carry[...] = cum[tm - 1:tm, :]
    parts = jnp.concatenate(_split3(cum * LOG2E), axis=1)
    kaug = _dot(parts, sk_ref[...]) + ck1_ref[...]
    qaug = _dot(parts, sq_ref[...]) + cq1_ref[...]
    for g in range(KV_C):
        kg = ck if g == 0 else pltpu.roll(ck, LANES - DC, 1)
        kc_ref[:, g * LANES:(g + 1) * LANES] = (jnp.where(low, kg, 0.0) + kaug[:, g * LANES:(g + 1) * LANES]).astype(BF16)
    for h in range(H_C):
        q = slab(O_CQ, h // 2)
        q = q if h % 2 == 0 else pltpu.roll(q, LANES - DC, 1)
        qc_ref[:, h * LANES:(h + 1) * LANES] = (jnp.where(low, q * (DC ** -0.5 * LOG2E), 0.0)
                                                + qaug[:, h * LANES:(h + 1) * LANES]).astype(BF16)


def _inproj(h, norm_g, p, l, tab_a, tab_b, tab_bt, tm, seq_tiles):
    t = h.shape[0]
    n_pos = tab_a.shape[1] // tm

    def full(a):
        nd = a.ndim - 1
        return pl.BlockSpec((None,) + a.shape[1:], lambda i: (l,) + (0,) * nd)

    def const(a):
        return pl.BlockSpec(a.shape, lambda i: (0,) * a.ndim)

    def rows(w):
        return pl.BlockSpec((tm, w), lambda i: (i, 0))

    def cols(w):
        return pl.BlockSpec((w, tm), lambda i: (0, i))

    tab_spec = pl.BlockSpec((3, tm, LANES), lambda i: (0, i % n_pos, 0))
    tabt_spec = pl.BlockSpec((2, DR_B // 2, tm), lambda i: (0, 0, i % n_pos))
    nseq = t // (seq_tiles * tm)

    def seq_cols(w):
        return pl.BlockSpec((None, w, tm), lambda i: (i // seq_tiles, 0, i % seq_tiles))

    def seq_shape(w):
        return jax.ShapeDtypeStruct((nseq, w, seq_tiles * tm), F32)

    out_shape = [jax.ShapeDtypeStruct((KV_A * t, 2 * DA), F32), jax.ShapeDtypeStruct((KV_A * t, 2 * DA), F32),
                 jax.ShapeDtypeStruct((t, KV_RANK_B), F32), seq_shape(DR_B), seq_shape(KV_C * DC), seq_shape(KV_C * DC),
                 seq_shape(H_C)]
    out_specs = [pl.BlockSpec((KV_A * tm, 2 * DA), lambda i: (i, 0)), pl.BlockSpec((KV_A * tm, 2 * DA), lambda i: (i, 0)),
                 rows(KV_RANK_B), seq_cols(DR_B), seq_cols(KV_C * DC), seq_cols(KV_C * DC), seq_cols(H_C)]
    bf_w = ((1024, 0), (256, 0), (KV_A * VA_ROWS, 1), (H_B * 256, 0), (256, 0), (VB_ROWS, 1), (1024, 0), (256, 0),
            (KV_C * VC_ROWS, 1))
    out_shape += [jax.ShapeDtypeStruct((w, t) if tr else (t, w), BF16) for w, tr in bf_w]
    out_specs += [cols(w) if tr else rows(w) for w, tr in bf_w]
    names = ("w_main", "w_t", "b_f", "q_norm", "kv_norm", "kv_norm_col", "w_uq", "w_uk_bd")
    consts = (p["sel_k"], p["sel_q"], p["one_k"], p["one_q"])
    return pl.pallas_call(
        functools.partial(_inproj_body, seq_tiles=seq_tiles),
        grid=(t // tm,),
        in_specs=([rows(D_MODEL), full(norm_g)] + [full(p[n]) for n in names]
                  + [tab_spec, tab_spec, tabt_spec, full(p["b_f_col"])] + [const(a) for a in consts]),
        out_specs=out_specs,
        out_shape=out_shape,
        scratch_shapes=[pltpu.VMEM((1, LANES), F32)],
        compiler_params=_cparams(("arbitrary",)),
        name="inproj",
    )(h, norm_g, *[p[n] for n in names], tab_a, tab_b, tab_bt, p["b_f_col"], *consts)


def _flash_pass(streams, k_ref, vt_ref, qi):
    def step(r0, nk, carries, diagonal):
        out = []
        for (m, acc), (q_stack, kcols, vrows) in zip(carries, streams):
            st = _dot_nt(k_ref[pl.ds(r0, nk), kcols], q_stack)
            if diagonal:
                key = lax.broadcasted_iota(jnp.int32, st.shape, 0)
                qry = lax.broadcasted_iota(jnp.int32, st.shape, 1) & (TQ - 1)
                st = jnp.where(key <= qry, st, NEG)
            m_new = jnp.maximum(m, jnp.max(st, axis=0, keepdims=True))
            pt = jnp.exp2(st - m_new).astype(BF16)
            acc = jnp.exp2(m - m_new) * acc + _dot(vt_ref[vrows, pl.ds(r0, nk)], pt)
            out.append((m_new, acc))
        return tuple(out)

    def init(q_stack, kcols, vrows):
        w = q_stack.shape[0]
        return (jnp.full((1, w), NEG, F32), jnp.zeros((vrows.stop - vrows.start, w), F32))

    pairs = qi // 2
    carries = lax.fori_loop(0, pairs, lambda j, c: step(pl.multiple_of(j * 2 * TQ, 2 * TQ), 2 * TQ, c, False),
                            tuple(init(*s) for s in streams))
    carries = lax.fori_loop(0, qi % 2, lambda j, c: step(pl.multiple_of(pairs * 2 * TQ, 2 * TQ), TQ, c, False), carries)
    outs = []
    for (m, acc), (q_stack, kcols, vrows) in zip(step(pl.multiple_of(qi * TQ, TQ), TQ, carries, True), streams):
        dv = vrows.stop - vrows.start - ONES_ROWS
        outs.append(acc[:dv] * (1.0 / acc[dv:dv + 1]))
    return outs


def _flash_heads(q_ref, width, n_heads, kcols, vrows, k_ref, vt_ref, qi, hpp=HEADS_PER_PASS):
    streams = []
    for first in range(0, n_heads, hpp):
        slabs = [q_ref[:, (first + v) * width:(first + v + 1) * width] for v in range(hpp)]
        streams.append((slabs[0] if len(slabs) == 1 else jnp.concatenate(slabs, axis=0), kcols(first), vrows(first)))
    outs = _flash_pass(streams, k_ref, vt_ref, qi)
    return [ot[:, v * TQ:(v + 1) * TQ] for ot in outs for v in range(hpp)]


def _flash_a_body(q_ref, k_ref, vt_ref, dl_ref, sub_ref, o_ref, *, lam_init):
    qi = pl.program_id(1)
    dl = dl_ref[...]
    lam = (jnp.exp(jnp.sum(dl[0:1] * dl[1:2], axis=-1, keepdims=True))
           - jnp.exp(jnp.sum(dl[2:3] * dl[3:4], axis=-1, keepdims=True)) + lam_init)
    def group(slab):
        return slab // (2 * H_A // KV_A)

    outs = _flash_heads(q_ref, LANES, 2 * H_A, lambda s: slice(group(s) * 2 * DA, (group(s) + 1) * 2 * DA),
                        lambda s: slice(group(s) * VA_ROWS, (group(s) + 1) * VA_ROWS), k_ref, vt_ref, qi)
    for h in range(H_A):
        y = outs[2 * h] - lam * outs[2 * h + 1]
        y = y * lax.rsqrt(jnp.mean(y * y, axis=0, keepdims=True) + EPS) * sub_ref[...] * (1.0 - lam_init)
        o_ref[h * 2 * DA:(h + 1) * 2 * DA, :] = y.astype(BF16)


def _flash_b_body(q_ref, k_ref, vt_ref, wuvt_ref, o_ref):
    qi = pl.program_id(1)
    outs = _flash_heads(q_ref, 256, H_B, lambda h: slice(0, 256), lambda h: slice(0, VB_ROWS), k_ref, vt_ref, qi,
                        hpp=H_B)
    ybt = _dot(wuvt_ref[...], jnp.concatenate([o.astype(BF16) for o in outs], axis=0))
    o_ref[...] = ybt.astype(BF16)


def _flash_c_body(q_ref, k_ref, vt_ref, o_ref):
    qi = pl.program_id(1)
    per = H_C // KV_C
    outs = _flash_heads(q_ref, LANES, H_C, lambda h: slice((h // per) * LANES, (h // per + 1) * LANES),
                        lambda h: slice((h // per) * VC_ROWS, (h // per + 1) * VC_ROWS), k_ref, vt_ref, qi)
    o_ref[...] = jnp.concatenate([o.astype(BF16) for o in outs], axis=0)


def _flash_call(body, name, batch, seq, q, k, vt, extra, extra_specs):
    t = q.shape[0]
    nq = seq // TQ
    return pl.pallas_call(
        body,
        grid=(batch, nq),
        in_specs=[pl.BlockSpec((TQ, q.shape[1]), lambda b, i: (b * nq + i, 0)),
                  pl.BlockSpec((seq, k.shape[1]), lambda b, i: (b, 0)),
                  pl.BlockSpec((vt.shape[0], seq), lambda b, i: (0, b))] + extra_specs,
        out_specs=pl.BlockSpec((BRANCH_W, TQ), lambda b, i: (0, b * nq + i)),
        out_shape=jax.ShapeDtypeStruct((BRANCH_W, t), BF16),
        compiler_params=_cparams(("parallel", "arbitrary")),
        name=name,
    )(q, k, vt, *extra)


def _decode_body(pt_ref, *refs, l, lam_init, n_steps, n_pages, n_total):
    n = PAGES_PER_STEP
    hbm = refs[:N_CACHES]
    (qa_ref, ka_ref, va_ref, ql_ref, qr_ref, kl_ref, krn_ref, qc_ref, kc_ref, vc_ref, lfn_ref,
     dl_ref, sub_ref, wuv_ref) = refs[N_CACHES:N_CACHES + 14]
    oa_ref, ob_ref, oc_ref = refs[N_CACHES + 14:N_CACHES + 17]
    bufs = refs[N_CACHES + 17:2 * N_CACHES + 17]
    sems = refs[2 * N_CACHES + 17]
    ma, la, acca, mb, lb, accb, mc, lc, accc, carry, qrb_ref, qcb_ref = refs[2 * N_CACHES + 18:]
    ak_b, av_b, lat_b, kr_b, ck_b, cv_b, lf_b = bufs
    b = pl.program_id(0)
    step = pl.program_id(1)
    t = b * n_steps + step
    slot = lax.rem(t, 2)

    def page_copy(c, page, sl, i):
        return pltpu.make_async_copy(hbm[c].at[l, page], bufs[c].at[sl, i], sems.at[sl, c])

    def start_step(bb, ss, sl):
        for i in range(n):
            page = pt_ref[bb, n_pages - 1 - (ss * n + i)]
            for c in range(N_CACHES):
                page_copy(c, page, sl, i).start()

    @pl.when(t == 0)
    def _():
        start_step(0, 0, 0)

    @pl.when(t + 1 < n_total)
    def _():
        last = step == n_steps - 1
        start_step(jnp.where(last, b + 1, b), jnp.where(last, 0, step + 1), 1 - slot)

    for i in range(n):
        for c in range(N_CACHES):
            page_copy(c, 0, slot, i).wait()

    qa = qa_ref[...]
    ql = ql_ref[...]
    qr = qr_ref[...]
    qc = qc_ref[...]

    @pl.when(step == 0)
    def _():
        ma[...] = jnp.sum(qa * ka_ref[...], axis=-1, keepdims=True)
        la[...] = jnp.ones_like(la)
        acca[...] = va_ref[...]
        mb[...] = (jnp.sum(ql * kl_ref[...], axis=-1, keepdims=True)
                   + jnp.sum(qr * krn_ref[...], axis=-1, keepdims=True))
        lb[...] = jnp.ones_like(lb)
        accb[...] = jnp.broadcast_to(kl_ref[...], accb.shape)
        mc[...] = jnp.sum(qc * kc_ref[...], axis=-1, keepdims=True)
        lc[...] = jnp.ones_like(lc)
        accc[...] = vc_ref[...]
        carry[...] = lfn_ref[...]
        qrb_ref[...] = jnp.zeros_like(qrb_ref)
        qcb_ref[...] = jnp.zeros_like(qcb_ref)
        for i in range(n):
            qrb_ref[i * 8:(i + 1) * 8, i * DR_B:(i + 1) * DR_B] = qr
            qcb_ref[i * 8:(i + 1) * 8, i * KV_C * DC:(i + 1) * KV_C * DC] = qc

    def update(s, m_ref, l_ref, acc_ref, pv):
        m_old = m_ref[...]
        m_new = jnp.maximum(m_old, jnp.max(s, axis=-1, keepdims=True))
        alpha = jnp.exp2(m_old - m_new)
        pr = jnp.exp2(s - m_new)
        l_ref[...] = alpha * l_ref[...] + jnp.sum(pr, axis=-1, keepdims=True)
        acc_ref[...] = alpha * acc_ref[...] + pv(pr)
        m_ref[...] = m_new

    def pages_to_lanes(x):
        return jnp.concatenate([x[i * 8:(i + 1) * 8] for i in range(n)], axis=1)

    def pages_to_rows(x):
        return jnp.concatenate([x[:, i * PAGE:(i + 1) * PAGE] for i in range(n)], axis=0)

    ak_all = ak_b[slot].reshape(n * KV_A * PAGE, 2 * DA)
    av_all = av_b[slot].reshape(n * KV_A * PAGE, 2 * DA)
    row_a = lax.broadcasted_iota(jnp.int32, (8, n * KV_A * PAGE), 0)
    col_a = lax.broadcasted_iota(jnp.int32, (8, n * KV_A * PAGE), 1)
    s_a = jnp.where((col_a & 1) == (row_a >> 2), _dot_nt(qa, ak_all), NEG)
    update(s_a, ma, la, acca, lambda pr: _dot(pr, av_all))

    lat_all = lat_b[slot].reshape(n * PAGE, KV_RANK_B)
    s_b = _dot_nt(ql, lat_all) + pages_to_lanes(_dot(qrb_ref[...], kr_b[slot].reshape(n * DR_B, PAGE)))
    update(s_b, mb, lb, accb, lambda pr: _dot(pr, lat_all))

    lane = lax.broadcasted_iota(jnp.int32, (H_C, PAGE), 1)
    bias = []
    run = carry[...]
    for i in range(n):
        x = lf_b[slot, i]
        suf = x
        sh = 1
        while sh < PAGE:
            suf = suf + jnp.where(lane + sh < PAGE, pltpu.roll(suf, PAGE - sh, 1), 0.0)
            sh *= 2
        bias.append(((suf - x) + run) * LOG2E)
        run = run + suf[:, 0:1]
    carry[...] = run
    s_c = (pages_to_lanes(_dot(qcb_ref[...], ck_b[slot].reshape(n * KV_C * DC, PAGE)))
           + jnp.concatenate(bias, axis=1))
    cv_all = cv_b[slot].reshape(n * KV_C * DC, PAGE)

    def pv_c(pr):
        big = _dot_nt(pages_to_rows(pr), cv_all)
        w = KV_C * DC
        return sum(big[i * 8:(i + 1) * 8, i * w:(i + 1) * w] for i in range(n))

    update(s_c, mc, lc, accc, pv_c)

    @pl.when(step == n_steps - 1)
    def _():
        dl = dl_ref[...]
        lam = (jnp.exp(jnp.sum(dl[0:1] * dl[1:2], axis=-1, keepdims=True))
               - jnp.exp(jnp.sum(dl[2:3] * dl[3:4], axis=-1, keepdims=True)) + lam_init)
        oa = acca[...] / la[...]
        for h in range(H_A):
            y = oa[2 * h:2 * h + 1] - lam * oa[2 * h + 1:2 * h + 2]
            oa_ref[h:h + 1, :] = _rms(y, sub_ref[...]) * (1.0 - lam_init)
        ob = (accb[...] / lb[...]).astype(BF16)
        row = lax.broadcasted_iota(jnp.int32, (H_B, DV_B), 0)
        yb = jnp.zeros((H_B, DV_B), F32)
        for h in range(H_B):
            yb = jnp.where(row == h, _dot(ob, wuv_ref[h]), yb)
        ob_ref[...] = yb
        oc = accc[...] / lc[...]
        oc_ref[...] = jnp.where(row < H_C // KV_C, oc[:, :DC], oc[:, DC:])


def _decode(l, page_table, caches, samp, p, lam_init):
    nb, n_pages = page_table.shape
    n = PAGES_PER_STEP
    n_steps = n_pages // n

    def samp_spec(a):
        return pl.BlockSpec((None,) + a.shape[1:], lambda b, s, pt: (b,) + (0,) * (a.ndim - 1))

    def layer_spec(a):
        return pl.BlockSpec((None,) + a.shape[1:], lambda b, s, pt: (l,) + (0,) * (a.ndim - 1))

    layer_args = (p["diff_lambda"], p["diff_subln"], p["w_uv_h"])
    in_specs = ([pl.BlockSpec(memory_space=pl.ANY)] * N_CACHES + [samp_spec(a) for a in samp]
                + [layer_spec(a) for a in layer_args])
    out_shape = [jax.ShapeDtypeStruct((nb, H_A, 2 * DA), F32), jax.ShapeDtypeStruct((nb, H_B, DV_B), F32),
                 jax.ShapeDtypeStruct((nb, H_C, DC), F32)]
    out_specs = [pl.BlockSpec((None,) + s.shape[1:], lambda b, s_, pt: (b, 0, 0)) for s in out_shape]
    col = pltpu.VMEM((8, 1), F32)
    scratch = ([pltpu.VMEM((2, n) + a.shape[2:], F32) for a in caches] + [pltpu.SemaphoreType.DMA((2, N_CACHES))]
               + [col, col, pltpu.VMEM((8, 2 * DA), F32), col, col, pltpu.VMEM((8, KV_RANK_B), F32),
                  col, col, pltpu.VMEM((8, KV_C * DC), F32), col,
                  pltpu.VMEM((n * H_B, n * DR_B), F32), pltpu.VMEM((n * H_C, n * KV_C * DC), F32)])
    grid_spec = pltpu.PrefetchScalarGridSpec(
        num_scalar_prefetch=1, grid=(nb, n_steps), in_specs=in_specs, out_specs=out_specs, scratch_shapes=scratch)
    return pl.pallas_call(
        functools.partial(_decode_body, l=l, lam_init=lam_init, n_steps=n_steps, n_pages=n_pages,
                          n_total=nb * n_steps),
        grid_spec=grid_spec,
        out_shape=out_shape,
        compiler_params=_cparams(("arbitrary", "arbitrary")),
        name="decode_attn",
    )(page_table, *caches, *samp, *layer_args)


def _merge_body(h_ref, g_ref, ya_ref, yb_ref, yc_ref, wgate_ref, wbr_ref, wout_ref, o_ref):
    h = h_ref[...]
    hn = _rms(h, g_ref[2:3, :]).astype(BF16)
    merged = None
    for n, y_ref in enumerate((ya_ref, yb_ref, yc_ref)):
        gate = jax.nn.sigmoid(_dot(hn, wgate_ref[:, n * D_MODEL:(n + 1) * D_MODEL]))
        term = gate * lax.dot_general(y_ref[...], wbr_ref[n], (((0,), (0,)), ((), ())), preferred_element_type=F32)
        merged = term if merged is None else merged + term
    m = _dot(merged.astype(BF16), wout_ref[...])
    o_ref[...] = h + _rms(m, g_ref[3:4, :])


def _merge(h, norm_g, ya, yb, yc, p, l, tm):
    t = h.shape[0]

    def full(a):
        nd = a.ndim - 1
        return pl.BlockSpec((None,) + a.shape[1:], lambda i: (l,) + (0,) * nd)

    def rows(w):
        return pl.BlockSpec((tm, w), lambda i: (i, 0))

    return pl.pallas_call(
        _merge_body,
        grid=(t // tm,),
        in_specs=([rows(D_MODEL), full(norm_g)] + [pl.BlockSpec((BRANCH_W, tm), lambda i: (0, i))] * N_BRANCH
                  + [full(p["w_gate"]), full(p["w_branch"]), full(p["w_out"])]),
        out_specs=rows(D_MODEL),
        out_shape=jax.ShapeDtypeStruct((t, D_MODEL), F32),
        compiler_params=_cparams(("parallel",)),
        name="merge",
    )(h, norm_g, ya, yb, yc, p["w_gate"], p["w_branch"], p["w_out"])


def _rope_tables(pos, rot, period, theta):
    half = rot // 2
    inv = 1.0 / (theta ** (jnp.arange(0, rot, 2, dtype=F32) / rot))
    ang = pos.astype(F32)[:, None] * inv[None, :]
    cos, sin = jnp.cos(ang), jnp.sin(ang)
    n = pos.shape[0]
    pad = jnp.zeros((n, period - rot), F32)
    c = jnp.concatenate([cos, cos, jnp.ones((n, period - rot), F32)], axis=1)
    s1 = jnp.concatenate([-sin, jnp.zeros((n, half), F32), pad], axis=1)
    s2 = jnp.concatenate([jnp.zeros((n, half), F32), sin, pad], axis=1)
    rep = LANES // period
    return jnp.stack([jnp.tile(c, (1, rep)), jnp.tile(s1, (1, rep)), jnp.tile(s2, (1, rep))])


def _rope_tables_t(pos, rot, theta):
    inv = 1.0 / (theta ** (jnp.arange(0, rot, 2, dtype=F32) / rot))
    ang = pos.astype(F32)[:, None] * inv[None, :]
    return jnp.stack([jnp.cos(ang).T, jnp.sin(ang).T])


def _bias_selectors():
    per = H_C // KV_C
    sel_k = np.zeros((3 * LANES, KV_C * LANES), np.float32)
    sel_q = np.zeros((3 * LANES, H_C * LANES), np.float32)
    one_k = np.zeros((1, KV_C * LANES), np.float32)
    one_q = np.zeros((1, H_C * LANES), np.float32)
    for h in range(H_C):
        g, hh = divmod(h, per)
        for part in range(3):
            sel_k[part * LANES + h, g * LANES + AUG_K0 + 3 * hh + part] = -1.0
            sel_q[part * LANES + h, h * LANES + AUG_Q0 + part] = 1.0
            one_q[0, h * LANES + AUG_K0 + 3 * hh + part] = 1.0
    for g in range(KV_C):
        for part in range(3):
            one_k[0, g * LANES + AUG_Q0 + part] = 1.0
    return jnp.asarray(sel_k, BF16), jnp.asarray(sel_q, BF16), jnp.asarray(one_k), jnp.asarray(one_q)


def _prep_params(w_in, b_forget, diff_lambda, diff_subln, mla_q_norm, mla_kv_norm, mla_w_uq, mla_w_uk, mla_w_uv,
                 w_branch, w_out):
    depth = w_in.shape[0]
    widths = (H_A * 2 * DA, KV_A * 2 * DA, KV_A * 2 * DA, Q_RANK_B, KV_RANK_B, DR_B, H_C * DC, KV_C * DC, KV_C * DC,
              H_C, N_BRANCH * D_MODEL)
    offs = np.cumsum((0,) + widths)
    seg = {n: w_in[:, :, offs[i]:offs[i + 1]] for i, n in enumerate(
        ("a_q", "a_k", "a_v", "b_cq", "b_ckv", "b_kr", "c_q", "c_k", "c_v", "c_f", "gate"))}
    zeros = jnp.zeros((depth, D_MODEL, LANES - H_C), w_in.dtype)
    w_main = jnp.concatenate(
        [seg["a_q"], seg["a_k"], seg["a_v"], seg["b_cq"], seg["b_ckv"], seg["c_k"], seg["c_q"], seg["c_v"],
         seg["b_kr"], seg["b_kr"], seg["b_kr"], seg["b_kr"], seg["c_f"], zeros], axis=2).astype(BF16)
    assert w_main.shape[2] == N_MAIN
    w_t = jnp.transpose(jnp.concatenate([seg["a_v"], seg["c_v"], seg["b_ckv"], seg["c_k"], seg["b_kr"], seg["c_f"]],
                                        axis=2), (0, 2, 1)).astype(BF16)
    assert w_t.shape[1] == ZT_ROWS
    uq = mla_w_uq.reshape(depth, Q_RANK_B, H_B, DN_B + DR_B)
    w_uq = jnp.concatenate([uq[..., :DN_B].reshape(depth, Q_RANK_B, H_B * DN_B),
                            uq[..., DN_B:].reshape(depth, Q_RANK_B, H_B * DR_B)], axis=2).astype(BF16)
    eye = jnp.eye(H_B, dtype=w_in.dtype)
    w_uk_bd = jnp.einsum("lchn,hg->lhngc", mla_w_uk, eye).reshape(depth, H_B * DN_B, H_B * KV_RANK_B).astype(BF16)
    w_uv_bdt = jnp.einsum("lchv,hg->lgvhc", mla_w_uv, eye).reshape(depth, H_B * DV_B, H_B * KV_RANK_B).astype(BF16)
    w_uv_h = jnp.transpose(mla_w_uv, (0, 2, 1, 3)).astype(BF16)
    sel_k, sel_q, one_k, one_q = _bias_selectors()
    return {
        "w_main": w_main,
        "w_t": w_t,
        "b_f_col": b_forget[:, :, None],
        "b_f": jnp.pad(b_forget, ((0, 0), (0, LANES - H_C)))[:, None, :],
        "q_norm": mla_q_norm[:, None, :],
        "kv_norm": mla_kv_norm[:, None, :],
        "kv_norm_col": mla_kv_norm[:, :, None],
        "w_uq": w_uq,
        "w_uk_bd": w_uk_bd,
        "w_uv_bdt": w_uv_bdt,
        "w_uv_h": w_uv_h,
        "w_gate": seg["gate"].astype(BF16),
        "w_branch": w_branch.astype(BF16),
        "w_out": w_out.astype(BF16),
        "diff_lambda": diff_lambda,
        "diff_subln": diff_subln[:, None, :],
        "diff_subln_col": diff_subln[:, :, None],
        "sel_k": sel_k, "sel_q": sel_q, "one_k": one_k, "one_q": one_q,
    }


def kernel(x_prompt, x_sample, cache_a_k, cache_a_v, cache_b_lat, cache_b_kr, cache_c_k, cache_c_v, cache_c_logf, page_table, norm_g, ffn_w_gate, ffn_w_up, ffn_w_down, w_in, b_forget, diff_lambda, diff_subln, mla_q_norm, mla_kv_norm, mla_w_uq, mla_w_uk, mla_w_uv, w_branch, w_out):
    batch, seq, _ = x_prompt.shape
    nb = x_sample.shape[0]
    depth = norm_g.shape[0]
    past_len = page_table.shape[1] * PAGE
    tp = batch * seq
    tm_p = 512

    p = _prep_params(w_in, b_forget, diff_lambda, diff_subln, mla_q_norm, mla_kv_norm, mla_w_uq, mla_w_uk, mla_w_uv,
                     w_branch, w_out)
    wg, wu, wd = ffn_w_gate.astype(BF16), ffn_w_up.astype(BF16), ffn_w_down.astype(BF16)
    pos_p = jnp.arange(seq, dtype=jnp.int32)
    pos_s = jnp.full((nb,), past_len, dtype=jnp.int32)
    tabs_p = (_rope_tables(pos_p, ROT_A, DA, THETA_A), _rope_tables(pos_p, DR_B, DR_B, THETA_B))
    tabs_s = (_rope_tables(pos_s, ROT_A, DA, THETA_A), _rope_tables(pos_s, DR_B, DR_B, THETA_B))
    tabt_p, tabt_s = _rope_tables_t(pos_p, DR_B, THETA_B), _rope_tables_t(pos_s, DR_B, THETA_B)
    n_pool = cache_a_k.shape[1]
    caches = (cache_a_k.reshape(depth, n_pool, KV_A * PAGE, 2 * DA),
              cache_a_v.reshape(depth, n_pool, KV_A * PAGE, 2 * DA),
              cache_b_lat,
              jnp.transpose(cache_b_kr, (0, 1, 3, 2)),
              jnp.transpose(cache_c_k, (0, 1, 3, 4, 2)).reshape(depth, n_pool, KV_C * DC, PAGE),
              jnp.transpose(cache_c_v, (0, 1, 3, 4, 2)).reshape(depth, n_pool, KV_C * DC, PAGE),
              jnp.transpose(cache_c_logf, (0, 1, 3, 2)))

    xp = x_prompt.reshape(tp, D_MODEL)
    xs = x_sample.reshape(nb, D_MODEL)
    rows_p = [[] for _ in range(7)]
    rows_s = [[] for _ in range(7)]
    for l in range(depth):
        lam_init = 0.8 - 0.6 * math.exp(-0.3 * l)
        lay = lambda a: pl.BlockSpec((None,) + a.shape[1:], lambda b, i: (l,) + (0,) * (a.ndim - 1))
        hs = _ffn(xs, norm_g, wg, wu, wd, l, 0, nb)
        (ak, av, ckv, krt, ckt, cvtf, lft, qa, akb, avt, qb, kb, ckvt, qc, kc, cvt) = _inproj(
            hs, norm_g, p, l, tabs_s[0], tabs_s[1], tabt_s, nb, 1)
        kr, ck, cv, lf = krt[0].T, ckt[0].T, cvtf[0].T, lft[0].T
        rep_a = 2 * H_A // KV_A
        qa3 = qa.astype(F32).reshape(nb, 2 * H_A, 2 * DA)
        ka = jnp.repeat(akb.astype(F32).reshape(nb, KV_A, 2 * DA), rep_a, axis=1)
        va = jnp.repeat(av.reshape(nb, KV_A, 2 * DA), rep_a, axis=1)
        qb3 = qb.astype(F32).reshape(nb, H_B, 256)
        ql = qb3[:, :, :KV_RANK_B]
        qr = qb3[:, :, KV_RANK_B:].reshape(nb, H_B, LANES // DR_B, DR_B).sum(axis=2)
        kbf = kb.astype(F32)
        kl = kbf[:, None, :KV_RANK_B]
        krn = kbf[:, None, KV_RANK_B:KV_RANK_B + DR_B]
        qc1 = qc.astype(F32).reshape(nb, H_C, LANES)[:, :, :DC]
        in_g0 = (jnp.arange(H_C) < H_C // KV_C)[None, :, None]
        qc3 = jnp.concatenate([jnp.where(in_g0, qc1, 0.0), jnp.where(in_g0, 0.0, qc1)], axis=2)
        kc3 = jnp.tile(jnp.repeat(ck.astype(BF16).astype(F32).reshape(nb, KV_C, DC), H_C // KV_C, axis=1), (1, 1, KV_C))
        vc3 = jnp.tile(jnp.repeat(cv.reshape(nb, KV_C, DC), H_C // KV_C, axis=1), (1, 1, KV_C))
        lfn = lf.reshape(nb, H_C, 1)
        oa, ob, oc = _decode(l, page_table, caches, (qa3, ka, va, ql, qr, kl, krn, qc3, kc3, vc3, lfn), p, lam_init)
        h2 = _merge(hs, norm_g, oa.reshape(nb, BRANCH_W).astype(BF16).T, ob.reshape(nb, BRANCH_W).astype(BF16).T,
                    oc.reshape(nb, BRANCH_W).astype(BF16).T, p, l, nb)
        xs = _ffn(h2, norm_g, wg, wu, wd, l, 1, nb)
        for dst, a in zip(rows_s, (ak.reshape(nb, 1, KV_A, 2 * DA), av.reshape(nb, 1, KV_A, 2 * DA),
                                   ckv.reshape(nb, 1, KV_RANK_B), kr.reshape(nb, 1, DR_B),
                                   ck.reshape(nb, 1, KV_C, DC), cv.reshape(nb, 1, KV_C, DC),
                                   lf.reshape(nb, 1, H_C))):
            dst.append(a)

        hp = _ffn(xp, norm_g, wg, wu, wd, l, 0, tm_p)
        (ak, av, ckv, krt, ckt, cvtf, lft, qa, akb, avt, qb, kb, ckvt, qc, kc, cvt) = _inproj(
            hp, norm_g, p, l, tabs_p[0], tabs_p[1], tabt_p, tm_p, seq // tm_p)
        ya = _flash_call(functools.partial(_flash_a_body, lam_init=lam_init), "flash_a", batch, seq, qa, akb, avt,
                         (p["diff_lambda"], p["diff_subln_col"]), [lay(p["diff_lambda"]), lay(p["diff_subln_col"])])
        yb = _flash_call(_flash_b_body, "flash_b", batch, seq, qb, kb, ckvt, (p["w_uv_bdt"],), [lay(p["w_uv_bdt"])])
        yc = _flash_call(_flash_c_body, "flash_c", batch, seq, qc, kc, cvt, (), [])
        h2 = _merge(hp, norm_g, ya, yb, yc, p, l, tm_p)
        xp = _ffn(h2, norm_g, wg, wu, wd, l, 1, tm_p)
        for dst, a in zip(rows_p, (ak.reshape(batch, seq, KV_A, 2 * DA), av.reshape(batch, seq, KV_A, 2 * DA),
                                   ckv.reshape(batch, seq, KV_RANK_B), jnp.transpose(krt, (0, 2, 1)),
                                   jnp.transpose(ckt.reshape(batch, KV_C, DC, seq), (0, 3, 1, 2)),
                                   jnp.transpose(cvtf.reshape(batch, KV_C, DC, seq), (0, 3, 1, 2)),
                                   jnp.transpose(lft, (0, 2, 1)))):
            dst.append(a)

    out = [xp.reshape(batch, seq, D_MODEL), xs.reshape(nb, 1, D_MODEL)]
    for rp, rs in zip(rows_p, rows_s):
        out.append(jnp.stack(rp))
        out.append(jnp.stack(rs))
    return tuple(out)
```

```python
import functools
import math

import numpy as np
import jax
import jax.numpy as jnp
from jax import lax
from jax.experimental import pallas as pl
from jax.experimental.pallas import tpu as pltpu

F32 = jnp.float32
BF16 = jnp.bfloat16

D_MODEL = 1024
D_FF = 2816
EPS = 1e-6
PAGE = 128
H_A, KV_A, DA, ROT_A, THETA_A = 4, 2, 64, 16, 500000.0
H_B, Q_RANK_B, KV_RANK_B, DN_B, DR_B, DV_B, THETA_B = 8, 256, 128, 64, 32, 64, 10000.0
H_C, KV_C, DC = 8, 2, 64
BRANCH_W = 512
N_BRANCH = 3
LANES = 128
NEG = -1e30
LOG2E = math.log2(math.e)

O_AQ, O_AK, O_AV, O_BCQ, O_CKV, O_CK, O_CQ, O_CV, O_KR, O_CF, N_MAIN = (
    0, 512, 768, 1024, 1280, 1408, 1536, 2048, 2176, 2304, 2432)
AUG_K0 = DC
AUG_Q0 = DC + 3 * (H_C // KV_C)
ZT_CV, ZT_CKV, ZT_CK, ZT_KR, ZT_CF, ZT_ROWS = 256, 384, 512, 640, 672, 680

FF_CHUNK = 1408
TQ = 512
HEADS_PER_PASS = 4
VMEM_LIMIT = 56 * 1024 * 1024
PAGES_PER_STEP = 16
N_CACHES = 7
ONES_ROWS = 16
VA_ROWS, VB_ROWS, VC_ROWS = 2 * DA + ONES_ROWS, KV_RANK_B + ONES_ROWS, DC + ONES_ROWS


def _cparams(sem):
    return pltpu.CompilerParams(dimension_semantics=sem, vmem_limit_bytes=VMEM_LIMIT)


def _rms(x, g):
    return x * lax.rsqrt(jnp.mean(x * x, axis=-1, keepdims=True) + EPS) * g


def _log_sigmoid(x):
    return jnp.minimum(x, 0.0) - jnp.log(1.0 + jnp.exp(-jnp.abs(x)))


def _dot(a, b):
    return jnp.dot(a, b, preferred_element_type=F32)


def _dot_nt(a, b):
    return lax.dot_general(a, b, (((1,), (1,)), ((), ())), preferred_element_type=F32)


def _ffn_body(x_ref, g_ref, wg_ref, wu_ref, wd_ref, o_ref, *, pre, post):
    x = x_ref[...]
    xn = _rms(x, g_ref[pre:pre + 1, :]).astype(BF16)
    acc = None
    for c in range(D_FF // FF_CHUNK):
        sl = slice(c * FF_CHUNK, (c + 1) * FF_CHUNK)
        g = _dot(xn, wg_ref[:, sl])
        u = _dot(xn, wu_ref[:, sl])
        h = (g * jax.nn.sigmoid(g) * u).astype(BF16)
        d = _dot(h, wd_ref[sl, :])
        acc = d if acc is None else acc + d
    o_ref[...] = x + 0.5 * _rms(acc, g_ref[post:post + 1, :])


def _ffn(x, norm_g, wg, wu, wd, l, j, tm):
    t = x.shape[0]
    once = pl.Buffered(1)
    return pl.pallas_call(
        functools.partial(_ffn_body, pre=3 * j + (0 if j == 0 else 1), post=3 * j + (1 if j == 0 else 2)),
        grid=(t // tm,),
        in_specs=[
            pl.BlockSpec((tm, D_MODEL), lambda i: (i, 0)),
            pl.BlockSpec((None, 6, D_MODEL), lambda i: (l, 0, 0)),
            pl.BlockSpec((None, None, D_MODEL, D_FF), lambda i: (l, j, 0, 0), pipeline_mode=once),
            pl.BlockSpec((None, None, D_MODEL, D_FF), lambda i: (l, j, 0, 0), pipeline_mode=once),
            pl.BlockSpec((None, None, D_FF, D_MODEL), lambda i: (l, j, 0, 0), pipeline_mode=once),
        ],
        out_specs=pl.BlockSpec((tm, D_MODEL), lambda i: (i, 0)),
        out_shape=jax.ShapeDtypeStruct((t, D_MODEL), F32),
        compiler_params=_cparams(("parallel",)),
        name="ffn",
    )(x, norm_g, wg, wu, wd)


def _rope(x, tab_ref, half):
    return (x * tab_ref[0]
            + pltpu.roll(x, LANES - half, 1) * tab_ref[1]
            + pltpu.roll(x, half, 1) * tab_ref[2])


def _split3(x):
    hi = x.astype(BF16)
    r1 = x - hi.astype(F32)
    mid = r1.astype(BF16)
    lo = (r1 - mid.astype(F32)).astype(BF16)
    return hi, mid, lo


def _inproj_body(h_ref, g_ref, wm_ref, wt_ref, bf_ref, qn_ref, kvn_ref, kvnc_ref, wuq_ref, wuk_ref,
                 ta_ref, tb_ref, tbt_ref, bfc_ref, sk_ref, sq_ref, ck1_ref, cq1_ref,
                 ak_ref, av_ref, ckv_ref, krt_ref, ckt_ref, cvtf_ref, lft_ref,
                 qa_ref, akb_ref, avt_ref, qb_ref, kb_ref, ckvt_ref, qc_ref, kc_ref, cvt_ref,
                 carry, *, seq_tiles):
    tm = h_ref.shape[0]
    hn = _rms(h_ref[...], g_ref[2:3, :]).astype(BF16)
    z = _dot(hn, wm_ref[...])
    zt = _dot_nt(wt_ref[...], hn)
    lane = lax.broadcasted_iota(jnp.int32, (1, LANES), 1)
    low = lane < DC

    def slab(off, i=0):
        return z[:, off + i * LANES: off + (i + 1) * LANES]

    for h in range(H_A):
        q = _rope(slab(O_AQ, h), ta_ref, ROT_A // 2) * (DA ** -0.5 * LOG2E)
        qa_ref[:, (2 * h) * LANES:(2 * h + 1) * LANES] = jnp.where(low, q, 0.0).astype(BF16)
        qa_ref[:, (2 * h + 1) * LANES:(2 * h + 2) * LANES] = jnp.where(low, 0.0, q).astype(BF16)
    for i in range(2):
        k = _rope(slab(O_AK, i), ta_ref, ROT_A // 2)
        ak_ref[pl.ds(i, tm, stride=KV_A), :] = k
        akb_ref[:, i * LANES:(i + 1) * LANES] = k.astype(BF16)
        av_ref[pl.ds(i, tm, stride=KV_A), :] = slab(O_AV, i)
    ones = jnp.ones((ONES_ROWS, tm), BF16)
    for g in range(KV_A):
        avt_ref[g * VA_ROWS:g * VA_ROWS + 2 * DA, :] = zt[g * 2 * DA:(g + 1) * 2 * DA].astype(BF16)
        avt_ref[g * VA_ROWS + 2 * DA:(g + 1) * VA_ROWS, :] = ones

    qc = _rms(z[:, O_BCQ:O_BCQ + Q_RANK_B], qn_ref[...]).astype(BF16)
    qb = _dot(qc, wuq_ref[...])
    q_lat = _dot(qb[:, :H_B * DN_B].astype(BF16), wuk_ref[...])
    scale_b = (DN_B + DR_B) ** -0.5 * LOG2E
    for s in range(2):
        qr = _rope(qb[:, H_B * DN_B + s * LANES: H_B * DN_B + (s + 1) * LANES], tb_ref, DR_B // 2) * scale_b
        for hh in range(4):
            h = s * 4 + hh
            qb_ref[:, h * 256: h * 256 + LANES] = (q_lat[:, h * LANES:(h + 1) * LANES] * scale_b).astype(BF16)
            sel = (lane >= hh * DR_B) & (lane < (hh + 1) * DR_B)
            qb_ref[:, h * 256 + LANES: (h + 1) * 256] = jnp.where(sel, qr, 0.0).astype(BF16)
    ckv = _rms(slab(O_CKV), kvn_ref[...])
    ckv_ref[...] = ckv
    kr = _rope(slab(O_KR), tb_ref, DR_B // 2)
    kb_ref[:, :LANES] = ckv.astype(BF16)
    kb_ref[:, LANES:] = kr.astype(BF16)
    half = DR_B // 2
    x1, x2 = zt[ZT_KR:ZT_KR + half], zt[ZT_KR + half:ZT_KR + DR_B]
    krt_ref[0:half, :] = x1 * tbt_ref[0] - x2 * tbt_ref[1]
    krt_ref[half:, :] = x2 * tbt_ref[0] + x1 * tbt_ref[1]
    xt = zt[ZT_CKV:ZT_CKV + KV_RANK_B]
    ckvt_ref[0:KV_RANK_B, :] = (xt * lax.rsqrt(jnp.mean(xt * xt, axis=0, keepdims=True) + EPS)
                                * kvnc_ref[...]).astype(BF16)
    ckvt_ref[KV_RANK_B:, :] = ones

    ck = slab(O_CK)
    ckt_ref[...] = zt[ZT_CK:ZT_CK + KV_C * DC]
    cvtf_ref[...] = zt[ZT_CV:ZT_CV + KV_C * DC]
    for g in range(KV_C):
        cvt_ref[g * VC_ROWS:g * VC_ROWS + DC, :] = zt[ZT_CV + g * DC:ZT_CV + (g + 1) * DC].astype(BF16)
        cvt_ref[g * VC_ROWS + DC:(g + 1) * VC_ROWS, :] = ones
    lf = _log_sigmoid(slab(O_CF) + bf_ref[...])
    lft_ref[...] = _log_sigmoid(zt[ZT_CF:ZT_CF + H_C] + bfc_ref[...])

    @pl.when(pl.program_id(0) % seq_tiles == 0)
    def _():
        carry[...] = jnp.zeros_like(carry)

    row = lax.broadcasted_iota(jnp.int32, (tm, LANES), 0)
    cum = lf
    s = 1
    while s < tm:
        cum = cum + jnp.where(row >= s, pltpu.roll(cum, s, 0), 0.0)
        s *= 2
    cum = cum + carry[...]
    carry[...] = cum[tm - 1:tm, :]
    parts = jnp.concatenate(_split3(cum * LOG2E), axis=1)
    kaug = _dot(parts, sk_ref[...]) + ck1_ref[...]
    qaug = _dot(parts, sq_ref[...]) + cq1_ref[...]
    for g in range(KV_C):
        kg = ck if g == 0 else pltpu.roll(ck, LANES - DC, 1)
        kc_ref[:, g * LANES:(g + 1) * LANES] = (jnp.where(low, kg, 0.0) + kaug[:, g * LANES:(g + 1) * LANES]).astype(BF16)
    for h in range(H_C):
        q = slab(O_CQ, h // 2)
        q = q if h % 2 == 0 else pltpu.roll(q, LANES - DC, 1)
        qc_ref[:, h * LANES:(h + 1) * LANES] = (jnp.where(low, q * (DC ** -0.5 * LOG2E), 0.0)
                                                + qaug[:, h * LANES:(h + 1) * LANES]).astype(BF16)


def _inproj(h, norm_g, p, l, tab_a, tab_b, tab_bt, tm, seq_tiles):
    t = h.shape[0]
    n_pos = tab_a.shape[1] // tm

    def full(a):
        nd = a.ndim - 1
        return pl.BlockSpec((None,) + a.shape[1:], lambda i: (l,) + (0,) * nd)

    def const(a):
        return pl.BlockSpec(a.shape, lambda i: (0,) * a.ndim)

    def rows(w):
        return pl.BlockSpec((tm, w), lambda i: (i, 0))

    def cols(w):
        return pl.BlockSpec((w, tm), lambda i: (0, i))

    tab_spec = pl.BlockSpec((3, tm, LANES), lambda i: (0, i % n_pos, 0))
    tabt_spec = pl.BlockSpec((2, DR_B // 2, tm), lambda i: (0, 0, i % n_pos))
    nseq = t // (seq_tiles * tm)

    def seq_cols(w):
        return pl.BlockSpec((None, w, tm), lambda i: (i // seq_tiles, 0, i % seq_tiles))

    def seq_shape(w):
        return jax.ShapeDtypeStruct((nseq, w, seq_tiles * tm), F32)

    out_shape = [jax.ShapeDtypeStruct((KV_A * t, 2 * DA), F32), jax.ShapeDtypeStruct((KV_A * t, 2 * DA), F32),
                 jax.ShapeDtypeStruct((t, KV_RANK_B), F32), seq_shape(DR_B), seq_shape(KV_C * DC), seq_shape(KV_C * DC),
                 seq_shape(H_C)]
    out_specs = [pl.BlockSpec((KV_A * tm, 2 * DA), lambda i: (i, 0)), pl.BlockSpec((KV_A * tm, 2 * DA), lambda i: (i, 0)),
                 rows(KV_RANK_B), seq_cols(DR_B), seq_cols(KV_C * DC), seq_cols(KV_C * DC), seq_cols(H_C)]
    bf_w = ((1024, 0), (256, 0), (KV_A * VA_ROWS, 1), (H_B * 256, 0), (256, 0), (VB_ROWS, 1), (1024, 0), (256, 0),
            (KV_C * VC_ROWS, 1))
    out_shape += [jax.ShapeDtypeStruct((w, t) if tr else (t, w), BF16) for w, tr in bf_w]
    out_specs += [cols(w) if tr else rows(w) for w, tr in bf_w]
    names = ("w_main", "w_t", "b_f", "q_norm", "kv_norm", "kv_norm_col", "w_uq", "w_uk_bd")
    consts = (p["sel_k"], p["sel_q"], p["one_k"], p["one_q"])
    return pl.pallas_call(
        functools.partial(_inproj_body, seq_tiles=seq_tiles),
        grid=(t // tm,),
        in_specs=([rows(D_MODEL), full(norm_g)] + [full(p[n]) for n in names]
                  + [tab_spec, tab_spec, tabt_spec, full(p["b_f_col"])] + [const(a) for a in consts]),
        out_specs=out_specs,
        out_shape=out_shape,
        scratch_shapes=[pltpu.VMEM((1, LANES), F32)],
        compiler_params=_cparams(("arbitrary",)),
        name="inproj",
    )(h, norm_g, *[p[n] for n in names], tab_a, tab_b, tab_bt, p["b_f_col"], *consts)


def _flash_pass(streams, k_ref, vt_ref, qi):
    def step(r0, nk, carries, diagonal):
        out = []
        for (m, acc), (q_stack, kcols, vrows) in zip(carries, streams):
            st = _dot_nt(k_ref[pl.ds(r0, nk), kcols], q_stack)
            if diagonal:
                key = lax.broadcasted_iota(jnp.int32, st.shape, 0)
                qry = lax.broadcasted_iota(jnp.int32, st.shape, 1) & (TQ - 1)
                st = jnp.where(key <= qry, st, NEG)
            m_new = jnp.maximum(m, jnp.max(st, axis=0, keepdims=True))
            pt = jnp.exp2(st - m_new).astype(BF16)
            acc = jnp.exp2(m - m_new) * acc + _dot(vt_ref[vrows, pl.ds(r0, nk)], pt)
            out.append((m_new, acc))
        return tuple(out)

    def init(q_stack, kcols, vrows):
        w = q_stack.shape[0]
        return (jnp.full((1, w), NEG, F32), jnp.zeros((vrows.stop - vrows.start, w), F32))

    pairs = qi // 2
    carries = lax.fori_loop(0, pairs, lambda j, c: step(pl.multiple_of(j * 2 * TQ, 2 * TQ), 2 * TQ, c, False),
                            tuple(init(*s) for s in streams))
    carries = lax.fori_loop(0, qi % 2, lambda j, c: step(pl.multiple_of(pairs * 2 * TQ, 2 * TQ), TQ, c, False), carries)
    outs = []
    for (m, acc), (q_stack, kcols, vrows) in zip(step(pl.multiple_of(qi * TQ, TQ), TQ, carries, True), streams):
        dv = vrows.stop - vrows.start - ONES_ROWS
        outs.append(acc[:dv] * (1.0 / acc[dv:dv + 1]))
    return outs


def _flash_heads(q_ref, width, n_heads, kcols, vrows, k_ref, vt_ref, qi, hpp=HEADS_PER_PASS):
    streams = []
    for first in range(0, n_heads, hpp):
        slabs = [q_ref[:, (first + v) * width:(first + v + 1) * width] for v in range(hpp)]
        streams.append((slabs[0] if len(slabs) == 1 else jnp.concatenate(slabs, axis=0), kcols(first), vrows(first)))
    outs = _flash_pass(streams, k_ref, vt_ref, qi)
    return [ot[:, v * TQ:(v + 1) * TQ] for ot in outs for v in range(hpp)]


def _flash_a_body(q_ref, k_ref, vt_ref, dl_ref, sub_ref, o_ref, *, lam_init):
    qi = pl.program_id(1)
    dl = dl_ref[...]
    lam = (jnp.exp(jnp.sum(dl[0:1] * dl[1:2], axis=-1, keepdims=True))
           - jnp.exp(jnp.sum(dl[2:3] * dl[3:4], axis=-1, keepdims=True)) + lam_init)
    def group(slab):
        return slab // (2 * H_A // KV_A)

    outs = _flash_heads(q_ref, LANES, 2 * H_A, lambda s: slice(group(s) * 2 * DA, (group(s) + 1) * 2 * DA),
                        lambda s: slice(group(s) * VA_ROWS, (group(s) + 1) * VA_ROWS), k_ref, vt_ref, qi)
    for h in range(H_A):
        y = outs[2 * h] - lam * outs[2 * h + 1]
        y = y * lax.rsqrt(jnp.mean(y * y, axis=0, keepdims=True) + EPS) * sub_ref[...] * (1.0 - lam_init)
        o_ref[h * 2 * DA:(h + 1) * 2 * DA, :] = y.astype(BF16)


def _flash_b_body(q_ref, k_ref, vt_ref, wuvt_ref, o_ref):
    qi = pl.program_id(1)
    outs = _flash_heads(q_ref, 256, H_B, lambda h: slice(0, 256), lambda h: slice(0, VB_ROWS), k_ref, vt_ref, qi,
                        hpp=H_B)
    ybt = _dot(wuvt_ref[...], jnp.concatenate([o.astype(BF16) for o in outs], axis=0))
    o_ref[...] = ybt.astype(BF16)


def _flash_c_body(q_ref, k_ref, vt_ref, o_ref):
    qi = pl.program_id(1)
    per = H_C // KV_C
    outs = _flash_heads(q_ref, LANES, H_C, lambda h: slice((h // per) * LANES, (h // per + 1) * LANES),
                        lambda h: slice((h // per) * VC_ROWS, (h // per + 1) * VC_ROWS), k_ref, vt_ref, qi)
    o_ref[...] = jnp.concatenate([o.astype(BF16) for o in outs], axis=0)


def _flash_call(body, name, batch, seq, q, k, vt, extra, extra_specs):
    t = q.shape[0]
    nq = seq // TQ
    return pl.pallas_call(
        body,
        grid=(batch, nq),
        in_specs=[pl.BlockSpec((TQ, q.shape[1]), lambda b, i: (b * nq + i, 0)),
                  pl.BlockSpec((seq, k.shape[1]), lambda b, i: (b, 0)),
                  pl.BlockSpec((vt.shape[0], seq), lambda b, i: (0, b))] + extra_specs,
        out_specs=pl.BlockSpec((BRANCH_W, TQ), lambda b, i: (0, b * nq + i)),
        out_shape=jax.ShapeDtypeStruct((BRANCH_W, t), BF16),
        compiler_params=_cparams(("parallel", "arbitrary")),
        name=name,
    )(q, k, vt, *extra)


def _decode_body(pt_ref, *refs, l, lam_init, n_steps, n_pages, n_total):
    n = PAGES_PER_STEP
    hbm = refs[:N_CACHES]
    (qa_ref, ka_ref, va_ref, ql_ref, qr_ref, kl_ref, krn_ref, qc_ref, kc_ref, vc_ref, lfn_ref,
     dl_ref, sub_ref, wuv_ref) = refs[N_CACHES:N_CACHES + 14]
    oa_ref, ob_ref, oc_ref = refs[N_CACHES + 14:N_CACHES + 17]
    bufs = refs[N_CACHES + 17:2 * N_CACHES + 17]
    sems = refs[2 * N_CACHES + 17]
    ma, la, acca, mb, lb, accb, mc, lc, accc, carry, qrb_ref, qcb_ref = refs[2 * N_CACHES + 18:]
    ak_b, av_b, lat_b, kr_b, ck_b, cv_b, lf_b = bufs
    b = pl.program_id(0)
    step = pl.program_id(1)
    t = b * n_steps + step
    slot = lax.rem(t, 2)

    def page_copy(c, page, sl, i):
        return pltpu.make_async_copy(hbm[c].at[l, page], bufs[c].at[sl, i], sems.at[sl, c])

    def start_step(bb, ss, sl):
        for i in range(n):
            page = pt_ref[bb, n_pages - 1 - (ss * n + i)]
            for c in range(N_CACHES):
                page_copy(c, page, sl, i).start()

    @pl.when(t == 0)
    def _():
        start_step(0, 0, 0)

    @pl.when(t + 1 < n_total)
    def _():
        last = step == n_steps - 1
        start_step(jnp.where(last, b + 1, b), jnp.where(last, 0, step + 1), 1 - slot)

    for i in range(n):
        for c in range(N_CACHES):
            page_copy(c, 0, slot, i).wait()

    qa = qa_ref[...]
    ql = ql_ref[...]
    qr = qr_ref[...]
    qc = qc_ref[...]

    @pl.when(step == 0)
    def _():
        ma[...] = jnp.sum(qa * ka_ref[...], axis=-1, keepdims=True)
        la[...] = jnp.ones_like(la)
        acca[...] = va_ref[...]
        mb[...] = (jnp.sum(ql * kl_ref[...], axis=-1, keepdims=True)
                   + jnp.sum(qr * krn_ref[...], axis=-1, keepdims=True))
        lb[...] = jnp.ones_like(lb)
        accb[...] = jnp.broadcast_to(kl_ref[...], accb.shape)
        mc[...] = jnp.sum(qc * kc_ref[...], axis=-1, keepdims=True)
        lc[...] = jnp.ones_like(lc)
        accc[...] = vc_ref[...]
        carry[...] = lfn_ref[...]
        qrb_ref[...] = jnp.zeros_like(qrb_ref)
        qcb_ref[...] = jnp.zeros_like(qcb_ref)
        for i in range(n):
            qrb_ref[i * 8:(i + 1) * 8, i * DR_B:(i + 1) * DR_B] = qr
            qcb_ref[i * 8:(i + 1) * 8, i * KV_C * DC:(i + 1) * KV_C * DC] = qc

    def update(s, m_ref, l_ref, acc_ref, pv):
        m_old = m_ref[...]
        m_new = jnp.maximum(m_old, jnp.max(s, axis=-1, keepdims=True))
        alpha = jnp.exp2(m_old - m_new)
        pr = jnp.exp2(s - m_new)
        l_ref[...] = alpha * l_ref[...] + jnp.sum(pr, axis=-1, keepdims=True)
        acc_ref[...] = alpha * acc_ref[...] + pv(pr)
        m_ref[...] = m_new

    def pages_to_lanes(x):
        return jnp.concatenate([x[i * 8:(i + 1) * 8] for i in range(n)], axis=1)

    def pages_to_rows(x):
        return jnp.concatenate([x[:, i * PAGE:(i + 1) * PAGE] for i in range(n)], axis=0)

    ak_all = ak_b[slot].reshape(n * KV_A * PAGE, 2 * DA)
    av_all = av_b[slot].reshape(n * KV_A * PAGE, 2 * DA)
    row_a = lax.broadcasted_iota(jnp.int32, (8, n * KV_A * PAGE), 0)
    col_a = lax.broadcasted_iota(jnp.int32, (8, n * KV_A * PAGE), 1)
    s_a = jnp.where((col_a & 1) == (row_a >> 2), _dot_nt(qa, ak_all), NEG)
    update(s_a, ma, la, acca, lambda pr: _dot(pr, av_all))

    lat_all = lat_b[slot].reshape(n * PAGE, KV_RANK_B)
    s_b = _dot_nt(ql, lat_all) + pages_to_lanes(_dot(qrb_ref[...], kr_b[slot].reshape(n * DR_B, PAGE)))
    update(s_b, mb, lb, accb, lambda pr: _dot(pr, lat_all))

    lane = lax.broadcasted_iota(jnp.int32, (H_C, PAGE), 1)
    bias = []
    run = carry[...]
    for i in range(n):
        x = lf_b[slot, i]
        suf = x
        sh = 1
        while sh < PAGE:
            suf = suf + jnp.where(lane + sh < PAGE, pltpu.roll(suf, PAGE - sh, 1), 0.0)
            sh *= 2
        bias.append(((suf - x) + run) * LOG2E)
        run = run + suf[:, 0:1]
    carry[...] = run
    s_c = (pages_to_lanes(_dot(qcb_ref[...], ck_b[slot].reshape(n * KV_C * DC, PAGE)))
           + jnp.concatenate(bias, axis=1))
    cv_all = cv_b[slot].reshape(n * KV_C * DC, PAGE)

    def pv_c(pr):
        big = _dot_nt(pages_to_rows(pr), cv_all)
        w = KV_C * DC
        return sum(big[i * 8:(i + 1) * 8, i * w:(i + 1) * w] for i in range(n))

    update(s_c, mc, lc, accc, pv_c)

    @pl.when(step == n_steps - 1)
    def _():
        dl = dl_ref[...]
        lam = (jnp.exp(jnp.sum(dl[0:1] * dl[1:2], axis=-1, keepdims=True))
               - jnp.exp(jnp.sum(dl[2:3] * dl[3:4], axis=-1, keepdims=True)) + lam_init)
        oa = acca[...] / la[...]
        for h in range(H_A):
            y = oa[2 * h:2 * h + 1] - lam * oa[2 * h + 1:2 * h + 2]
            oa_ref[h:h + 1, :] = _rms(y, sub_ref[...]) * (1.0 - lam_init)
        ob = (accb[...] / lb[...]).astype(BF16)
        row = lax.broadcasted_iota(jnp.int32, (H_B, DV_B), 0)
        yb = jnp.zeros((H_B, DV_B), F32)
        for h in range(H_B):
            yb = jnp.where(row == h, _dot(ob, wuv_ref[h]), yb)
        ob_ref[...] = yb
        oc = accc[...] / lc[...]
        oc_ref[...] = jnp.where(row < H_C // KV_C, oc[:, :DC], oc[:, DC:])


def _decode(l, page_table, caches, samp, p, lam_init):
    nb, n_pages = page_table.shape
    n = PAGES_PER_STEP
    n_steps = n_pages // n

    def samp_spec(a):
        return pl.BlockSpec((None,) + a.shape[1:], lambda b, s, pt: (b,) + (0,) * (a.ndim - 1))

    def layer_spec(a):
        return pl.BlockSpec((None,) + a.shape[1:], lambda b, s, pt: (l,) + (0,) * (a.ndim - 1))

    layer_args = (p["diff_lambda"], p["diff_subln"], p["w_uv_h"])
    in_specs = ([pl.BlockSpec(memory_space=pl.ANY)] * N_CACHES + [samp_spec(a) for a in samp]
                + [layer_spec(a) for a in layer_args])
    out_shape = [jax.ShapeDtypeStruct((nb, H_A, 2 * DA), F32), jax.ShapeDtypeStruct((nb, H_B, DV_B), F32),
                 jax.ShapeDtypeStruct((nb, H_C, DC), F32)]
    out_specs = [pl.BlockSpec((None,) + s.shape[1:], lambda b, s_, pt: (b, 0, 0)) for s in out_shape]
    col = pltpu.VMEM((8, 1), F32)
    scratch = ([pltpu.VMEM((2, n) + a.shape[2:], F32) for a in caches] + [pltpu.SemaphoreType.DMA((2, N_CACHES))]
               + [col, col, pltpu.VMEM((8, 2 * DA), F32), col, col, pltpu.VMEM((8, KV_RANK_B), F32),
                  col, col, pltpu.VMEM((8, KV_C * DC), F32), col,
                  pltpu.VMEM((n * H_B, n * DR_B), F32), pltpu.VMEM((n * H_C, n * KV_C * DC), F32)])
    grid_spec = pltpu.PrefetchScalarGridSpec(
        num_scalar_prefetch=1, grid=(nb, n_steps), in_specs=in_specs, out_specs=out_specs, scratch_shapes=scratch)
    return pl.pallas_call(
        functools.partial(_decode_body, l=l, lam_init=lam_init, n_steps=n_steps, n_pages=n_pages,
                          n_total=nb * n_steps),
        grid_spec=grid_spec,
        out_shape=out_shape,
        compiler_params=_cparams(("arbitrary", "arbitrary")),
        name="decode_attn",
    )(page_table, *caches, *samp, *layer_args)


def _merge_body(h_ref, g_ref, ya_ref, yb_ref, yc_ref, wgate_ref, wbr_ref, wout_ref, o_ref):
    h = h_ref[...]
    hn = _rms(h, g_ref[2:3, :]).astype(BF16)
    merged = None
    for n, y_ref in enumerate((ya_ref, yb_ref, yc_ref)):
        gate = jax.nn.sigmoid(_dot(hn, wgate_ref[:, n * D_MODEL:(n + 1) * D_MODEL]))
        term = gate * lax.dot_general(y_ref[...], wbr_ref[n], (((0,), (0,)), ((), ())), preferred_element_type=F32)
        merged = term if merged is None else merged + term
    m = _dot(merged.astype(BF16), wout_ref[...])
    o_ref[...] = h + _rms(m, g_ref[3:4, :])


def _merge(h, norm_g, ya, yb, yc, p, l, tm):
    t = h.shape[0]

    def full(a):
        nd = a.ndim - 1
        return pl.BlockSpec((None,) + a.shape[1:], lambda i: (l,) + (0,) * nd)

    def rows(w):
        return pl.BlockSpec((tm, w), lambda i: (i, 0))

    return pl.pallas_call(
        _merge_body,
        grid=(t // tm,),
        in_specs=([rows(D_MODEL), full(norm_g)] + [pl.BlockSpec((BRANCH_W, tm), lambda i: (0, i))] * N_BRANCH
                  + [full(p["w_gate"]), full(p["w_branch"]), full(p["w_out"])]),
        out_specs=rows(D_MODEL),
        out_shape=jax.ShapeDtypeStruct((t, D_MODEL), F32),
        compiler_params=_cparams(("parallel",)),
        name="merge",
    )(h, norm_g, ya, yb, yc, p["w_gate"], p["w_branch"], p["w_out"])


def _rope_tables(pos, rot, period, theta):
    half = rot // 2
    inv = 1.0 / (theta ** (jnp.arange(0, rot, 2, dtype=F32) / rot))
    ang = pos.astype(F32)[:, None] * inv[None, :]
    cos, sin = jnp.cos(ang), jnp.sin(ang)
    n = pos.shape[0]
    pad = jnp.zeros((n, period - rot), F32)
    c = jnp.concatenate([cos, cos, jnp.ones((n, period - rot), F32)], axis=1)
    s1 = jnp.concatenate([-sin, jnp.zeros((n, half), F32), pad], axis=1)
    s2 = jnp.concatenate([jnp.zeros((n, half), F32), sin, pad], axis=1)
    rep = LANES // period
    return jnp.stack([jnp.tile(c, (1, rep)), jnp.tile(s1, (1, rep)), jnp.tile(s2, (1, rep))])


def _rope_tables_t(pos, rot, theta):
    inv = 1.0 / (theta ** (jnp.arange(0, rot, 2, dtype=F32) / rot))
    ang = pos.astype(F32)[:, None] * inv[None, :]
    return jnp.stack([jnp.cos(ang).T, jnp.sin(ang).T])


def _bias_selectors():
    per = H_C // KV_C
    sel_k = np.zeros((3 * LANES, KV_C * LANES), np.float32)
    sel_q = np.zeros((3 * LANES, H_C * LANES), np.float32)
    one_k = np.zeros((1, KV_C * LANES), np.float32)
    one_q = np.zeros((1, H_C * LANES), np.float32)
    for h in range(H_C):
        g, hh = divmod(h, per)
        for part in range(3):
            sel_k[part * LANES + h, g * LANES + AUG_K0 + 3 * hh + part] = -1.0
            sel_q[part * LANES + h, h * LANES + AUG_Q0 + part] = 1.0
            one_q[0, h * LANES + AUG_K0 + 3 * hh + part] = 1.0
    for g in range(KV_C):
        for part in range(3):
            one_k[0, g * LANES + AUG_Q0 + part] = 1.0
    return jnp.asarray(sel_k, BF16), jnp.asarray(sel_q, BF16), jnp.asarray(one_k), jnp.asarray(one_q)


def _prep_params(w_in, b_forget, diff_lambda, diff_subln, mla_q_norm, mla_kv_norm, mla_w_uq, mla_w_uk, mla_w_uv,
                 w_branch, w_out):
    depth = w_in.shape[0]
    widths = (H_A * 2 * DA, KV_A * 2 * DA, KV_A * 2 * DA, Q_RANK_B, KV_RANK_B, DR_B, H_C * DC, KV_C * DC, KV_C * DC,
              H_C, N_BRANCH * D_MODEL)
    offs = np.cumsum((0,) + widths)
    seg = {n: w_in[:, :, offs[i]:offs[i + 1]] for i, n in enumerate(
        ("a_q", "a_k", "a_v", "b_cq", "b_ckv", "b_kr", "c_q", "c_k", "c_v", "c_f", "gate"))}
    zeros = jnp.zeros((depth, D_MODEL, LANES - H_C), w_in.dtype)
    w_main = jnp.concatenate(
        [seg["a_q"], seg["a_k"], seg["a_v"], seg["b_cq"], seg["b_ckv"], seg["c_k"], seg["c_q"], seg["c_v"],
         seg["b_kr"], seg["b_kr"], seg["b_kr"], seg["b_kr"], seg["c_f"], zeros], axis=2).astype(BF16)
    assert w_main.shape[2] == N_MAIN
    w_t = jnp.transpose(jnp.concatenate([seg["a_v"], seg["c_v"], seg["b_ckv"], seg["c_k"], seg["b_kr"], seg["c_f"]],
                                        axis=2), (0, 2, 1)).astype(BF16)
    assert w_t.shape[1] == ZT_ROWS
    uq = mla_w_uq.reshape(depth, Q_RANK_B, H_B, DN_B + DR_B)
    w_uq = jnp.concatenate([uq[..., :DN_B].reshape(depth, Q_RANK_B, H_B * DN_B),
                            uq[..., DN_B:].reshape(depth, Q_RANK_B, H_B * DR_B)], axis=2).astype(BF16)
    eye = jnp.eye(H_B, dtype=w_in.dtype)
    w_uk_bd = jnp.einsum("lchn,hg->lhngc", mla_w_uk, eye).reshape(depth, H_B * DN_B, H_B * KV_RANK_B).astype(BF16)
    w_uv_bdt = jnp.einsum("lchv,hg->lgvhc", mla_w_uv, eye).reshape(depth, H_B * DV_B, H_B * KV_RANK_B).astype(BF16)
    w_uv_h = jnp.transpose(mla_w_uv, (0, 2, 1, 3)).astype(BF16)
    sel_k, sel_q, one_k, one_q = _bias_selectors()
    return {
        "w_main": w_main,
        "w_t": w_t,
        "b_f_col": b_forget[:, :, None],
        "b_f": jnp.pad(b_forget, ((0, 0), (0, LANES - H_C)))[:, None, :],
        "q_norm": mla_q_norm[:, None, :],
        "kv_norm": mla_kv_norm[:, None, :],
        "kv_norm_col": mla_kv_norm[:, :, None],
        "w_uq": w_uq,
        "w_uk_bd": w_uk_bd,
        "w_uv_bdt": w_uv_bdt,
        "w_uv_h": w_uv_h,
        "w_gate": seg["gate"].astype(BF16),
        "w_branch": w_branch.astype(BF16),
        "w_out": w_out.astype(BF16),
        "diff_lambda": diff_lambda,
        "diff_subln": diff_subln[:, None, :],
        "diff_subln_col": diff_subln[:, :, None],
        "sel_k": sel_k, "sel_q": sel_q, "one_k": one_k, "one_q": one_q,
    }


def kernel(x_prompt, x_sample, cache_a_k, cache_a_v, cache_b_lat, cache_b_kr, cache_c_k, cache_c_v, cache_c_logf, page_table, norm_g, ffn_w_gate, ffn_w_up, ffn_w_down, w_in, b_forget, diff_lambda, diff_subln, mla_q_norm, mla_kv_norm, mla_w_uq, mla_w_uk, mla_w_uv, w_branch, w_out):
    batch, seq, _ = x_prompt.shape
    nb = x_sample.shape[0]
    depth = norm_g.shape[0]
    past_len = page_table.shape[1] * PAGE
    tp = batch * seq
    tm_p = 512

    p = _prep_params(w_in, b_forget, diff_lambda, diff_subln, mla_q_norm, mla_kv_norm, mla_w_uq, mla_w_uk, mla_w_uv,
                     w_branch, w_out)
    wg, wu, wd = ffn_w_gate.astype(BF16), ffn_w_up.astype(BF16), ffn_w_down.astype(BF16)
    pos_p = jnp.arange(seq, dtype=jnp.int32)
    pos_s = jnp.full((nb,), past_len, dtype=jnp.int32)
    tabs_p = (_rope_tables(pos_p, ROT_A, DA, THETA_A), _rope_tables(pos_p, DR_B, DR_B, THETA_B))
    tabs_s = (_rope_tables(pos_s, ROT_A, DA, THETA_A), _rope_tables(pos_s, DR_B, DR_B, THETA_B))
    tabt_p, tabt_s = _rope_tables_t(pos_p, DR_B, THETA_B), _rope_tables_t(pos_s, DR_B, THETA_B)
    n_pool = cache_a_k.shape[1]
    caches = (cache_a_k.reshape(depth, n_pool, KV_A * PAGE, 2 * DA),
              cache_a_v.reshape(depth, n_pool, KV_A * PAGE, 2 * DA),
              cache_b_lat,
              jnp.transpose(cache_b_kr, (0, 1, 3, 2)),
              jnp.transpose(cache_c_k, (0, 1, 3, 4, 2)).reshape(depth, n_pool, KV_C * DC, PAGE),
              jnp.transpose(cache_c_v, (0, 1, 3, 4, 2)).reshape(depth, n_pool, KV_C * DC, PAGE),
              jnp.transpose(cache_c_logf, (0, 1, 3, 2)))

    xp = x_prompt.reshape(tp, D_MODEL)
    xs = x_sample.reshape(nb, D_MODEL)
    rows_p = [[] for _ in range(7)]
    rows_s = [[] for _ in range(7)]
    for l in range(depth):
        lam_init = 0.8 - 0.6 * math.exp(-0.3 * l)
        lay = lambda a: pl.BlockSpec((None,) + a.shape[1:], lambda b, i: (l,) + (0,) * (a.ndim - 1))
        hs = _ffn(xs, norm_g, wg, wu, wd, l, 0, nb)
        (ak, av, ckv, krt, ckt, cvtf, lft, qa, akb, avt, qb, kb, ckvt, qc, kc, cvt) = _inproj(
            hs, norm_g, p, l, tabs_s[0], tabs_s[1], tabt_s, nb, 1)
        kr, ck, cv, lf = krt[0].T, ckt[0].T, cvtf[0].T, lft[0].T
        rep_a = 2 * H_A // KV_A
        qa3 = qa.astype(F32).reshape(nb, 2 * H_A, 2 * DA)
        ka = jnp.repeat(akb.astype(F32).reshape(nb, KV_A, 2 * DA), rep_a, axis=1)
        va = jnp.repeat(av.reshape(nb, KV_A, 2 * DA), rep_a, axis=1)
        qb3 = qb.astype(F32).reshape(nb, H_B, 256)
        ql = qb3[:, :, :KV_RANK_B]
        qr = qb3[:, :, KV_RANK_B:].reshape(nb, H_B, LANES // DR_B, DR_B).sum(axis=2)
        kbf = kb.astype(F32)
        kl = kbf[:, None, :KV_RANK_B]
        krn = kbf[:, None, KV_RANK_B:KV_RANK_B + DR_B]
        qc1 = qc.astype(F32).reshape(nb, H_C, LANES)[:, :, :DC]
        in_g0 = (jnp.arange(H_C) < H_C // KV_C)[None, :, None]
        qc3 = jnp.concatenate([jnp.where(in_g0, qc1, 0.0), jnp.where(in_g0, 0.0, qc1)], axis=2)
        kc3 = jnp.tile(jnp.repeat(ck.astype(BF16).astype(F32).reshape(nb, KV_C, DC), H_C // KV_C, axis=1), (1, 1, KV_C))
        vc3 = jnp.tile(jnp.repeat(cv.reshape(nb, KV_C, DC), H_C // KV_C, axis=1), (1, 1, KV_C))
        lfn = lf.reshape(nb, H_C, 1)
        oa, ob, oc = _decode(l, page_table, caches, (qa3, ka, va, ql, qr, kl, krn, qc3, kc3, vc3, lfn), p, lam_init)
        h2 = _merge(hs, norm_g, oa.reshape(nb, BRANCH_W).astype(BF16).T, ob.reshape(nb, BRANCH_W).astype(BF16).T,
                    oc.reshape(nb, BRANCH_W).astype(BF16).T, p, l, nb)
        xs = _ffn(h2, norm_g, wg, wu, wd, l, 1, nb)
        for dst, a in zip(rows_s, (ak.reshape(nb, 1, KV_A, 2 * DA), av.reshape(nb, 1, KV_A, 2 * DA),
                                   ckv.reshape(nb, 1, KV_RANK_B), kr.reshape(nb, 1, DR_B),
                                   ck.reshape(nb, 1, KV_C, DC), cv.reshape(nb, 1, KV_C, DC),
                                   lf.reshape(nb, 1, H_C))):
            dst.append(a)

        hp = _ffn(xp, norm_g, wg, wu, wd, l, 0, tm_p)
        (ak, av, ckv, krt, ckt, cvtf, lft, qa, akb, avt, qb, kb, ckvt, qc, kc, cvt) = _inproj(
            hp, norm_g, p, l, tabs_p[0], tabs_p[1], tabt_p, tm_p, seq // tm_p)
        ya = _flash_call(functools.partial(_flash_a_body, lam_init=lam_init), "flash_a", batch, seq, qa, akb, avt,
                         (p["diff_lambda"], p["diff_subln_col"]), [lay(p["diff_lambda"]), lay(p["diff_subln_col"])])
        yb = _flash_call(_flash_b_body, "flash_b", batch, seq, qb, kb, ckvt, (p["w_uv_bdt"],), [lay(p["w_uv_bdt"])])
        yc = _flash_call(_flash_c_body, "flash_c", batch, seq, qc, kc, cvt, (), [])
        h2 = _merge(hp, norm_g, ya, yb, yc, p, l, tm_p)
        xp = _ffn(h2, norm_g, wg, wu, wd, l, 1, tm_p)
        for dst, a in zip(rows_p, (ak.reshape(batch, seq, KV_A, 2 * DA), av.reshape(batch, seq, KV_A, 2 * DA),
                                   ckv.reshape(batch, seq, KV_RANK_B), jnp.transpose(krt, (0, 2, 1)),
                                   jnp.transpose(ckt.reshape(batch, KV_C, DC, seq), (0, 3, 1, 2)),
                                   jnp.transpose(cvtf.reshape(batch, KV_C, DC, seq), (0, 3, 1, 2)),
                                   jnp.transpose(lft, (0, 2, 1)))):
            dst.append(a)

    out = [xp.reshape(batch, seq, D_MODEL), xs.reshape(nb, 1, D_MODEL)]
    for rp, rs in zip(rows_p, rows_s):
        out.append(jnp.stack(rp))
        out.append(jnp.stack(rs))
    return tuple(out)
```
